```python
import math
import jax, jax.numpy as jnp
from jax import lax
import numpy as np


D_MODEL = 2048
BATCH = 8
SEQ = 4096
DEPTH = 2
DEC_BATCH = 1
DEC_SEQ = 16384
PAST_LEN = 128

HEAD_DIM = 128
N_HEADS = D_MODEL // HEAD_DIM
N_KV_HEADS = 4
GQA_GROUP = N_HEADS // N_KV_HEADS
ATTN_WIDTH = N_HEADS * HEAD_DIM
KV_WIDTH = N_KV_HEADS * HEAD_DIM
AXIS_DIM = HEAD_DIM // 2
ROPE_THETA = 10000.0
Q_BLOCK = 128
GRID_W = 64
POOL_WINDOWS = (2, 4, 8, 16)
POOL_GROUPS = len(POOL_WINDOWS)
POOL_WIDTH = D_MODEL // 2
POOL_GROUP_IN = POOL_WIDTH // POOL_GROUPS
POOL_GROUP_OUT = D_MODEL // POOL_GROUPS
N_BRANCHES = 2
IN_WIDTH = ATTN_WIDTH + 2 * KV_WIDTH + POOL_WIDTH + N_BRANCHES * D_MODEL
D_FF = 4 * D_MODEL
LN_EPS = 1e-5
QK_EPS = 1e-6
DEEPNORM_ALPHA = (2 * DEPTH) ** 0.25
DEEPNORM_BETA = (8 * DEPTH) ** -0.25

kernel_name = 'hybrid_gated_attn_pool_encoder'


def layer_norm(x, g, b):
    xf = x.astype(jnp.float32)
    mu = jnp.mean(xf, axis=-1, keepdims=True)
    xc = xf - mu
    var = jnp.mean(xc * xc, axis=-1, keepdims=True)
    y = xc * lax.rsqrt(var + LN_EPS) * g.astype(jnp.float32) + b.astype(jnp.float32)
    return y.astype(x.dtype)


def rms_norm_heads(x, g):
    xf = x.astype(jnp.float32)
    y = xf * lax.rsqrt(jnp.mean(xf * xf, axis=-1, keepdims=True) + QK_EPS) * g.astype(jnp.float32)
    return y.astype(x.dtype)


def axial_rope_angles(seq):
    rows = seq // GRID_W
    row = jnp.repeat(jnp.arange(rows, dtype=jnp.float32), GRID_W)
    col = jnp.tile(jnp.arange(GRID_W, dtype=jnp.float32), rows)
    inv_freq = ROPE_THETA ** (-jnp.arange(0, AXIS_DIM, 2, dtype=jnp.float32) / AXIS_DIM)
    return row[:, None] * inv_freq[None, :], col[:, None] * inv_freq[None, :]


def rotate_half(x, ang):
    xf = x.astype(jnp.float32)
    cos = jnp.cos(ang)[None, :, None, :]
    sin = jnp.sin(ang)[None, :, None, :]
    x1, x2 = jnp.split(xf, 2, axis=-1)
    return jnp.concatenate([x1 * cos - x2 * sin, x1 * sin + x2 * cos], axis=-1).astype(x.dtype)


def apply_axial_rope(x, ang_row, ang_col):
    xr, xc = jnp.split(x, 2, axis=-1)
    return jnp.concatenate([rotate_half(xr, ang_row), rotate_half(xc, ang_col)], axis=-1)


def blocked_gqa(q, k, v):
    b, s = q.shape[0], q.shape[1]
    nblk = s // Q_BLOCK
    scale = HEAD_DIM ** -0.5
    qb = q.reshape(b, nblk, Q_BLOCK, N_KV_HEADS, GQA_GROUP, HEAD_DIM).transpose(1, 0, 2, 3, 4, 5)

    def one_block(qblk):
        sc = jnp.einsum('bqhgd,bkhd->bhgqk', qblk, k, preferred_element_type=jnp.float32) * scale
        p = jax.nn.softmax(sc, axis=-1).astype(v.dtype)
        return jnp.einsum('bhgqk,bkhd->bqhgd', p, v)

    o = lax.map(one_block, qb)
    return o.transpose(1, 0, 2, 3, 4, 5).reshape(b, s, ATTN_WIDTH)


def multiscale_pool(u, w_pool, pool_scale):
    b, s, _ = u.shape
    uf = u.astype(jnp.float32)
    c = jnp.concatenate([jnp.zeros((b, 1, POOL_WIDTH), jnp.float32), jnp.cumsum(uf, axis=1)], axis=1)
    t = jnp.arange(s)
    outs = []
    for g, w in enumerate(POOL_WINDOWS):
        lo = jnp.maximum(t - w // 2, 0)
        hi = jnp.minimum(t + (w - 1 - w // 2), s - 1)
        sl = slice(g * POOL_GROUP_IN, (g + 1) * POOL_GROUP_IN)
        cg = c[..., sl]
        sums = jnp.take(cg, hi + 1, axis=1) - jnp.take(cg, lo, axis=1)
        cnt = (hi - lo + 1).astype(jnp.float32)[None, :, None]
        outs.append(sums / cnt - uf[..., sl])
    pooled = jnp.stack(outs, axis=2).astype(u.dtype)
    y = jnp.einsum('bsgc,gce->bsge', pooled, w_pool).reshape(b, s, D_MODEL)
    return y * pool_scale


def token_mixer(x, w_in, q_norm, k_norm, w_pool, pool_scale, w_o):
    b, s, _ = x.shape
    proj = x @ w_in
    o1 = ATTN_WIDTH
    o2 = o1 + KV_WIDTH
    o3 = o2 + KV_WIDTH
    o4 = o3 + POOL_WIDTH
    q, k, v, u, gl = jnp.split(proj, [o1, o2, o3, o4], axis=-1)
    q = rms_norm_heads(q.reshape(b, s, N_HEADS, HEAD_DIM), q_norm)
    k = rms_norm_heads(k.reshape(b, s, N_KV_HEADS, HEAD_DIM), k_norm)
    v = v.reshape(b, s, N_KV_HEADS, HEAD_DIM)
    ang_row, ang_col = axial_rope_angles(s)
    q = apply_axial_rope(q, ang_row, ang_col)
    k = apply_axial_rope(k, ang_row, ang_col)
    a = blocked_gqa(q, k, v)
    p = multiscale_pool(u, w_pool, pool_scale)
    gates = jax.nn.sigmoid(gl.reshape(b, s, N_BRANCHES, D_MODEL))
    m = gates[:, :, 0] * a + gates[:, :, 1] * p
    return m @ w_o


def sqrelu_mlp(x, w_up, w_down):
    h = jnp.square(jax.nn.relu(x @ w_up))
    return h @ w_down


def trunk(x, w_in, q_norm, k_norm, w_pool, pool_scale, w_o, ln1_g, ln1_b, w_up, w_down, ln2_g, ln2_b):
    for l in range(DEPTH):
        y = token_mixer(x, w_in[l], q_norm[l], k_norm[l], w_pool[l], pool_scale[l], w_o[l])
        x = layer_norm(DEEPNORM_ALPHA * x + y, ln1_g[l], ln1_b[l])
        y = sqrelu_mlp(x, w_up[l], w_down[l])
        x = layer_norm(DEEPNORM_ALPHA * x + y, ln2_g[l], ln2_b[l])
    return x


def setup_inputs(seed: int = 0) -> dict:
    key = jax.random.key(seed)
    ks = jax.random.split(key, 16)
    f32 = jnp.float32
    nrm = lambda k, shape, s: jax.random.normal(k, shape, f32) * s
    return {
        'x_prompt': nrm(ks[0], (BATCH, SEQ, D_MODEL), 1.0),
        'x_sample': nrm(ks[1], (DEC_BATCH, DEC_SEQ, D_MODEL), 1.0),
        'w_in': nrm(ks[2], (DEPTH, D_MODEL, IN_WIDTH), D_MODEL ** -0.5),
        'q_norm': 1.0 + nrm(ks[3], (DEPTH, HEAD_DIM), 0.02),
        'k_norm': 1.0 + nrm(ks[4], (DEPTH, HEAD_DIM), 0.02),
        'w_pool': nrm(ks[5], (DEPTH, POOL_GROUPS, POOL_GROUP_IN, POOL_GROUP_OUT), POOL_GROUP_IN ** -0.5),
        'pool_scale': 1.0 + nrm(ks[6], (DEPTH, D_MODEL), 0.05),
        'w_o': nrm(ks[7], (DEPTH, D_MODEL, D_MODEL), D_MODEL ** -0.5 * DEEPNORM_BETA),
        'ln1_g': 1.0 + nrm(ks[8], (DEPTH, D_MODEL), 0.02),
        'ln1_b': nrm(ks[9], (DEPTH, D_MODEL), 0.02),
        'w_up': nrm(ks[10], (DEPTH, D_MODEL, D_FF), D_MODEL ** -0.5),
        'w_down': nrm(ks[11], (DEPTH, D_FF, D_MODEL), D_FF ** -0.5 * DEEPNORM_BETA),
        'ln2_g': 1.0 + nrm(ks[12], (DEPTH, D_MODEL), 0.02),
        'ln2_b': nrm(ks[13], (DEPTH, D_MODEL), 0.02),
    }


def reference(x_prompt, x_sample, w_in, q_norm, k_norm, w_pool, pool_scale, w_o, ln1_g, ln1_b, w_up, w_down, ln2_g, ln2_b):
    y_prompt = trunk(x_prompt, w_in, q_norm, k_norm, w_pool, pool_scale, w_o, ln1_g, ln1_b, w_up, w_down, ln2_g, ln2_b)
    y_sample = trunk(x_sample, w_in, q_norm, k_norm, w_pool, pool_scale, w_o, ln1_g, ln1_b, w_up, w_down, ln2_g, ln2_b)
    return (y_prompt, y_sample)
```

```python
import functools
import math

import jax
import jax.numpy as jnp
from jax import lax
from jax.experimental import pallas as pl
from jax.experimental.pallas import tpu as pltpu

F32 = jnp.float32
BF16 = jnp.bfloat16

HEAD_DIM = 128
N_KV_HEADS = 4
GRID_W = 64
ROPE_THETA = 10000.0
POOL_WINDOWS = (2, 4, 8, 16)
POOL_HALO = 8
LN_EPS = 1e-5
QK_EPS = 1e-6
LOG2E = 1.4426950408889634
NEG_BIG = -1e30

VMEM_LIMIT_BYTES = 56 * 1024 * 1024


def _tile(n, pref):
    t = min(n, pref)
    assert n % t == 0, (n, pref)
    return t


def _params(sem):
    return pltpu.CompilerParams(dimension_semantics=sem,
                                vmem_limit_bytes=VMEM_LIMIT_BYTES)


def _rope_tables(seq):
    t = jnp.arange(seq)
    row = (t // GRID_W).astype(F32)
    col = (t % GRID_W).astype(F32)
    axis_dim = HEAD_DIM // 2
    inv_freq = ROPE_THETA ** (-jnp.arange(0, axis_dim, 2, dtype=F32) / axis_dim)
    ar = row[:, None] * inv_freq[None, :]
    ac = col[:, None] * inv_freq[None, :]
    cos = jnp.concatenate([jnp.cos(ar), jnp.cos(ar), jnp.cos(ac), jnp.cos(ac)], axis=-1)
    sin = jnp.concatenate([-jnp.sin(ar), jnp.sin(ar), -jnp.sin(ac), jnp.sin(ac)], axis=-1)
    return cos, sin


def _norm_rope(xh, gain, cos, sin, first_half):
    ms = jnp.mean(xh * xh, axis=-1, keepdims=True)
    y = xh * lax.rsqrt(ms + QK_EPS) * gain
    partner = jnp.where(first_half,
                        pltpu.roll(y, HEAD_DIM - 32, 1),
                        pltpu.roll(y, 32, 1))
    return y * cos + partner * sin


def _proj_rope_kernel(x_ref, w_ref, g_ref, cos_ref, sin_ref, o_ref, *, transposed):
    acc = jnp.dot(x_ref[...], w_ref[...], preferred_element_type=F32)
    tm = acc.shape[0]
    lane = lax.broadcasted_iota(jnp.int32, (tm, HEAD_DIM), 1)
    first_half = (lane % 64) < 32
    gain = g_ref[...]
    cos = cos_ref[...]
    sin = sin_ref[...]
    for h in range(acc.shape[1] // HEAD_DIM):
        r = _norm_rope(acc[:, h * HEAD_DIM:(h + 1) * HEAD_DIM], gain, cos, sin, first_half)
        if transposed:
            o_ref[h] = r.T.astype(o_ref.dtype)
        else:
            o_ref[:, h * HEAD_DIM:(h + 1) * HEAD_DIM] = r.astype(o_ref.dtype)


def _proj_plain_kernel(x_ref, w_ref, o_ref, *, mode):
    acc = jnp.dot(x_ref[...], w_ref[...], preferred_element_type=F32)
    if mode == "transposed":
        for h in range(acc.shape[1] // HEAD_DIM):
            o_ref[h] = acc[:, h * HEAD_DIM:(h + 1) * HEAD_DIM].T.astype(o_ref.dtype)
    elif mode == "sigmoid":
        o_ref[...] = (1.0 / (1.0 + jnp.exp(-acc))).astype(o_ref.dtype)
    else:
        o_ref[...] = acc.astype(o_ref.dtype)


def _proj(xb, w, *, mode, gain=None, cos=None, sin=None, out_dtype=F32, tm_pref=1024, tn=512):
    B, S, D = xb.shape
    N = w.shape[1]
    tm = _tile(S, tm_pref)
    tn = _tile(N, tn)
    grid = (B, S // tm, N // tn)
    hb = tn // HEAD_DIM
    x_spec = pl.BlockSpec((None, tm, D), lambda b, i, j: (b, i, 0))
    w_spec = pl.BlockSpec((D, tn), lambda b, i, j: (0, j))
    flat_out = pl.BlockSpec((None, tm, tn), lambda b, i, j: (b, i, j))
    head_out = pl.BlockSpec((None, hb, HEAD_DIM, tm), lambda b, i, j: (b, j, 0, i))
    if mode in ("q", "k"):
        transposed = mode == "q"
        kernel = functools.partial(_proj_rope_kernel, transposed=transposed)
        in_specs = [x_spec, w_spec,
                    pl.BlockSpec((1, HEAD_DIM), lambda b, i, j: (0, 0)),
                    pl.BlockSpec((tm, HEAD_DIM), lambda b, i, j: (i, 0)),
                    pl.BlockSpec((tm, HEAD_DIM), lambda b, i, j: (i, 0))]
        args = (xb, w, gain, cos, sin)
    else:
        transposed = mode == "v"
        kmode = {"v": "transposed", "gate": "sigmoid", "u": "plain"}[mode]
        kernel = functools.partial(_proj_plain_kernel, mode=kmode)
        in_specs = [x_spec, w_spec]
        args = (xb, w)
    if transposed:
        out_shape = jax.ShapeDtypeStruct((B, N // HEAD_DIM, HEAD_DIM, S), out_dtype)
        out_spec = head_out
    else:
        out_shape = jax.ShapeDtypeStruct((B, S, N), out_dtype)
        out_spec = flat_out
    return pl.pallas_call(
        kernel, grid=grid, in_specs=in_specs, out_specs=out_spec, out_shape=out_shape,
        compiler_params=_params(("parallel", "parallel", "arbitrary")),
        name="proj_" + mode,
    )(*args)


def _flash_kernel(qT_ref, k_ref, vT_ref, o_ref, m_ref, l_ref, acc_ref, *, tk, n_chunks, c2):
    G = qT_ref.shape[0]
    m_ref[...] = jnp.full(m_ref.shape, NEG_BIG, F32)
    l_ref[...] = jnp.zeros(l_ref.shape, F32)
    acc_ref[...] = jnp.zeros(acc_ref.shape, F32)

    def body(c, carry):
        off = pl.multiple_of(c * tk, tk)
        kc = k_ref[pl.ds(off, tk), :]
        vc = vT_ref[:, pl.ds(off, tk)]
        for g in range(G):
            s = jnp.dot(kc, qT_ref[g], preferred_element_type=F32)
            m_old = m_ref[g]
            m_new = jnp.maximum(m_old, jnp.max(s, axis=0, keepdims=True))
            alpha = jnp.exp2((m_old - m_new) * c2)
            p = jnp.exp2((s - m_new) * c2)
            l_ref[g] = alpha * l_ref[g] + jnp.sum(p, axis=0, keepdims=True)
            acc_ref[g] = alpha * acc_ref[g] + jnp.dot(
                vc, p.astype(BF16), preferred_element_type=F32)
            m_ref[g] = m_new
        return carry

    lax.fori_loop(0, n_chunks, body, 0)
    for g in range(G):
        o = acc_ref[g] / l_ref[g]
        o_ref[:, g * HEAD_DIM:(g + 1) * HEAD_DIM] = o.T.astype(o_ref.dtype)


def _flash(qT, k, vT, *, tq_pref=256, tk_pref=512):
    B, H, hd, S = qT.shape
    G = H // N_KV_HEADS
    tq = _tile(S, tq_pref)
    tk = _tile(S, tk_pref)
    c2 = (hd ** -0.5) * LOG2E
    kernel = functools.partial(_flash_kernel, tk=tk, n_chunks=S // tk, c2=c2)
    return pl.pallas_call(
        kernel,
        grid=(B, N_KV_HEADS, S // tq),
        in_specs=[
            pl.BlockSpec((None, G, hd, tq), lambda b, h, i: (b, h, 0, i)),
            pl.BlockSpec((None, S, hd), lambda b, h, i: (b, 0, h)),
            pl.BlockSpec((None, None, hd, S), lambda b, h, i: (b, h, 0, 0)),
        ],
        out_specs=pl.BlockSpec((None, tq, G * hd), lambda b, h, i: (b, i, h)),
        out_shape=jax.ShapeDtypeStruct((B, S, H * hd), F32),
        scratch_shapes=[
            pltpu.VMEM((G, 1, tq), F32),
            pltpu.VMEM((G, 1, tq), F32),
            pltpu.VMEM((G, hd, tq), F32),
        ],
        compiler_params=_params(("parallel", "parallel", "arbitrary")),
        name="flash_attn",
    )(qT, k, vT)


def _layer_norm(z, g, b):
    mu = jnp.mean(z, axis=-1, keepdims=True)
    zc = z - mu
    var = jnp.mean(zc * zc, axis=-1, keepdims=True)
    return zc * lax.rsqrt(var + LN_EPS) * g + b


def _mix_kernel(a_ref, gt_ref, u_ref, up_ref, un_ref, x_ref, wp_ref, ps_ref, wo_ref,
                g_ref, b_ref, o32_ref, o16_ref, pad_ref, *, alpha, seq):
    tm = u_ref.shape[0]
    d_model = a_ref.shape[1]
    n_groups = len(POOL_WINDOWS)
    gin = u_ref.shape[1] // n_groups
    gout = d_model // n_groups
    i = pl.program_id(1)
    nblk = pl.num_programs(1)
    pad_ref[0:POOL_HALO, :] = jnp.where(i > 0, up_ref[...], 0.0)
    pad_ref[POOL_HALO:POOL_HALO + tm, :] = u_ref[...]
    pad_ref[POOL_HALO + tm:, :] = jnp.where(i < nblk - 1, un_ref[...], 0.0)

    t = i * tm + lax.broadcasted_iota(jnp.int32, (tm, 1), 0)
    outs = []
    for g, w in enumerate(POOL_WINDOWS):
        lo_off = -(w // 2)
        hi_off = w - 1 - w // 2
        c0 = g * gin
        tot = None
        for d in range(lo_off, hi_off + 1):
            v = pad_ref[POOL_HALO + d:POOL_HALO + d + tm, c0:c0 + gin]
            tot = v if tot is None else tot + v
        lo = jnp.maximum(t + lo_off, 0)
        hi = jnp.minimum(t + hi_off, seq - 1)
        cnt = (hi - lo + 1).astype(F32)
        pooled = tot / cnt - u_ref[:, c0:c0 + gin]
        outs.append(jnp.dot(pooled.astype(BF16), wp_ref[g], preferred_element_type=F32))
    p = jnp.concatenate(outs, axis=-1) * ps_ref[...]
    m = gt_ref[:, :d_model] * a_ref[...] + gt_ref[:, d_model:] * p
    y = jnp.dot(m.astype(BF16), wo_ref[...], preferred_element_type=F32)
    out = _layer_norm(alpha * x_ref[...] + y, g_ref[...], b_ref[...])
    o32_ref[...] = out
    o16_ref[...] = out.astype(BF16)


def _mix(a, gates, u, x, w_pool, pool_scale, w_o, ln_g, ln_b, *, alpha, tm_pref=256):
    B, S, D = x.shape
    U = u.shape[-1]
    tm = _tile(S, tm_pref)
    hb = tm // POOL_HALO
    n_halo_blocks = S // POOL_HALO
    kernel = functools.partial(_mix_kernel, alpha=alpha, seq=S)
    row = lambda b, i: (b, i, 0)
    const2 = lambda b, i: (0, 0)
    return pl.pallas_call(
        kernel,
        grid=(B, S // tm),
        in_specs=[
            pl.BlockSpec((None, tm, D), row),
            pl.BlockSpec((None, tm, 2 * D), row),
            pl.BlockSpec((None, tm, U), row),
            pl.BlockSpec((None, POOL_HALO, U),
                         lambda b, i: (b, jnp.maximum(i * hb - 1, 0), 0)),
            pl.BlockSpec((None, POOL_HALO, U),
                         lambda b, i: (b, jnp.minimum((i + 1) * hb, n_halo_blocks - 1), 0)),
            pl.BlockSpec((None, tm, D), row),
            pl.BlockSpec(w_pool.shape, lambda b, i: (0, 0, 0)),
            pl.BlockSpec((1, D), const2),
            pl.BlockSpec((D, D), const2),
            pl.BlockSpec((1, D), const2),
            pl.BlockSpec((1, D), const2),
        ],
        out_specs=[pl.BlockSpec((None, tm, D), row), pl.BlockSpec((None, tm, D), row)],
        out_shape=[jax.ShapeDtypeStruct((B, S, D), F32), jax.ShapeDtypeStruct((B, S, D), BF16)],
        scratch_shapes=[pltpu.VMEM((tm + 2 * POOL_HALO, U), F32)],
        compiler_params=_params(("parallel", "arbitrary")),
        name="mix_wo_ln",
    )(a, gates, u, u, u, x, w_pool, pool_scale, w_o, ln_g, ln_b)


def _mlp_kernel(xb_ref, x_ref, wu_ref, wd_ref, g_ref, b_ref, o32_ref, o16_ref, acc_ref, *, alpha):
    f = pl.program_id(2)
    h = jnp.dot(xb_ref[...], wu_ref[...], preferred_element_type=F32)
    h = jnp.square(jnp.maximum(h, 0.0))
    contrib = jnp.dot(h.astype(BF16), wd_ref[...], preferred_element_type=F32)

    @pl.when(f == 0)
    def _():
        acc_ref[...] = contrib

    @pl.when(f > 0)
    def _():
        acc_ref[...] += contrib

    @pl.when(f == pl.num_programs(2) - 1)
    def _():
        out = _layer_norm(alpha * x_ref[...] + acc_ref[...], g_ref[...], b_ref[...])
        o32_ref[...] = out
        o16_ref[...] = out.astype(BF16)


def _mlp(xb, x, w_up, w_down, ln_g, ln_b, *, alpha, tm_pref=512, tf_pref=512):
    B, S, D = x.shape
    F = w_up.shape[1]
    tm = _tile(S, tm_pref)
    tf = _tile(F, tf_pref)
    kernel = functools.partial(_mlp_kernel, alpha=alpha)
    row = lambda b, i, f: (b, i, 0)
    const2 = lambda b, i, f: (0, 0)
    return pl.pallas_call(
        kernel,
        grid=(B, S // tm, F // tf),
        in_specs=[
            pl.BlockSpec((None, tm, D), row),
            pl.BlockSpec((None, tm, D), row),
            pl.BlockSpec((D, tf), lambda b, i, f: (0, f)),
            pl.BlockSpec((tf, D), lambda b, i, f: (f, 0)),
            pl.BlockSpec((1, D), const2),
            pl.BlockSpec((1, D), const2),
        ],
        out_specs=[pl.BlockSpec((None, tm, D), row), pl.BlockSpec((None, tm, D), row)],
        out_shape=[jax.ShapeDtypeStruct((B, S, D), F32), jax.ShapeDtypeStruct((B, S, D), BF16)],
        scratch_shapes=[pltpu.VMEM((tm, D), F32)],
        compiler_params=_params(("parallel", "parallel", "arbitrary")),
        name="mlp_ln",
    )(xb, x, w_up, w_down, ln_g, ln_b)


def _trunk(x, w):
    B, S, D = x.shape
    depth = w["w_in"].shape[0]
    alpha = (2 * depth) ** 0.25
    attn_w = D
    kv_w = N_KV_HEADS * HEAD_DIM
    pool_w = D // 2
    o1 = attn_w
    o2 = o1 + kv_w
    o3 = o2 + kv_w
    o4 = o3 + pool_w
    cos, sin = _rope_tables(S)
    xb = x.astype(BF16)
    for l in range(depth):
        w_in = w["w_in"][l]
        qT = _proj(xb, w_in[:, :o1], mode="q", gain=w["q_norm"][l], cos=cos, sin=sin, out_dtype=BF16)
        k = _proj(xb, w_in[:, o1:o2], mode="k", gain=w["k_norm"][l], cos=cos, sin=sin, out_dtype=BF16)
        vT = _proj(xb, w_in[:, o2:o3], mode="v", out_dtype=BF16)
        u = _proj(xb, w_in[:, o3:o4], mode="u")
        gates = _proj(xb, w_in[:, o4:], mode="gate")
        a = _flash(qT, k, vT)
        x, xb = _mix(a, gates, u, x, w["w_pool"][l], w["pool_scale"][l], w["w_o"][l],
                     w["ln1_g"][l], w["ln1_b"][l], alpha=alpha)
        x, xb = _mlp(xb, x, w["w_up"][l], w["w_down"][l], w["ln2_g"][l], w["ln2_b"][l], alpha=alpha)
    return x


def kernel(x_prompt, x_sample, w_in, q_norm, k_norm, w_pool, pool_scale, w_o,
           ln1_g, ln1_b, w_up, w_down, ln2_g, ln2_b):
    depth = w_in.shape[0]
    row = lambda v: v.reshape(depth, 1, v.shape[-1])
    w = {
        "w_in": w_in.astype(BF16),
        "q_norm": row(q_norm), "k_norm": row(k_norm),
        "w_pool": w_pool.astype(BF16), "pool_scale": row(pool_scale),
        "w_o": w_o.astype(BF16),
        "ln1_g": row(ln1_g), "ln1_b": row(ln1_b),
        "w_up": w_up.astype(BF16), "w_down": w_down.astype(BF16),
        "ln2_g": row(ln2_g), "ln2_b": row(ln2_b),
    }
    return (_trunk(x_prompt, w), _trunk(x_sample, w))
```

```python
import functools
import math

import jax
import jax.numpy as jnp
from jax import lax
from jax.experimental import pallas as pl
from jax.experimental.pallas import tpu as pltpu

F32 = jnp.float32
BF16 = jnp.bfloat16

HEAD_DIM = 128
N_KV_HEADS = 4
GRID_W = 64
ROPE_THETA = 10000.0
POOL_WINDOWS = (2, 4, 8, 16)
POOL_HALO = 8
LN_EPS = 1e-5
QK_EPS = 1e-6
LOG2E = 1.4426950408889634
SOFTMAX_SCALE_LOG2 = HEAD_DIM ** -0.5 * LOG2E
NEG_BIG = -1e30

VMEM_LIMIT_BYTES = 56 * 1024 * 1024


def _tile(n, pref):
    t = min(n, pref)
    assert n % t == 0, (n, pref)
    return t


def _params(sem):
    return pltpu.CompilerParams(dimension_semantics=sem,
                                vmem_limit_bytes=VMEM_LIMIT_BYTES)


def _rope_tables(seq):
    t = jnp.arange(seq)
    row = (t // GRID_W).astype(F32)
    col = (t % GRID_W).astype(F32)
    axis_dim = HEAD_DIM // 2
    inv_freq = ROPE_THETA ** (-jnp.arange(0, axis_dim, 2, dtype=F32) / axis_dim)
    ar = row[:, None] * inv_freq[None, :]
    ac = col[:, None] * inv_freq[None, :]
    cos = jnp.concatenate([jnp.cos(ar), jnp.cos(ar), jnp.cos(ac), jnp.cos(ac)], axis=-1)
    sin = jnp.concatenate([-jnp.sin(ar), jnp.sin(ar), -jnp.sin(ac), jnp.sin(ac)], axis=-1)
    return cos, sin


def _norm_rope(xh, gain, cos, sin, first_half):
    ms = jnp.mean(xh * xh, axis=-1, keepdims=True)
    y = xh * lax.rsqrt(ms + QK_EPS) * gain
    partner = jnp.where(first_half,
                        pltpu.roll(y, HEAD_DIM - 32, 1),
                        pltpu.roll(y, 32, 1))
    return y * cos + partner * sin


def _proj_rope_kernel(x_ref, w_ref, g_ref, cos_ref, sin_ref, o_ref, *, transposed, out_scale):
    acc = jnp.dot(x_ref[...], w_ref[...], preferred_element_type=F32)
    tm = acc.shape[0]
    lane = lax.broadcasted_iota(jnp.int32, (tm, HEAD_DIM), 1)
    first_half = (lane % 64) < 32
    gain = g_ref[...]
    cos = cos_ref[...] * out_scale
    sin = sin_ref[...] * out_scale
    for h in range(acc.shape[1] // HEAD_DIM):
        r = _norm_rope(acc[:, h * HEAD_DIM:(h + 1) * HEAD_DIM], gain, cos, sin, first_half)
        if transposed:
            o_ref[h] = r.T.astype(o_ref.dtype)
        else:
            o_ref[:, h * HEAD_DIM:(h + 1) * HEAD_DIM] = r.astype(o_ref.dtype)


def _proj_plain_kernel(x_ref, w_ref, o_ref, *, mode):
    acc = jnp.dot(x_ref[...], w_ref[...], preferred_element_type=F32)
    if mode == "transposed":
        for h in range(acc.shape[1] // HEAD_DIM):
            o_ref[h] = acc[:, h * HEAD_DIM:(h + 1) * HEAD_DIM].T.astype(o_ref.dtype)
    elif mode == "sigmoid":
        o_ref[...] = (1.0 / (1.0 + jnp.exp(-acc))).astype(o_ref.dtype)
    else:
        o_ref[...] = acc.astype(o_ref.dtype)


def _proj(xb, w, *, mode, gain=None, cos=None, sin=None, out_dtype=F32, tm_pref=1024, tn=512):
    B, S, D = xb.shape
    N = w.shape[1]
    tm = _tile(S, tm_pref)
    tn = _tile(N, tn)
    grid = (B, S // tm, N // tn)
    hb = tn // HEAD_DIM
    x_spec = pl.BlockSpec((None, tm, D), lambda b, i, j: (b, i, 0))
    w_spec = pl.BlockSpec((D, tn), lambda b, i, j: (0, j))
    flat_out = pl.BlockSpec((None, tm, tn), lambda b, i, j: (b, i, j))
    head_out = pl.BlockSpec((None, hb, HEAD_DIM, tm), lambda b, i, j: (b, j, 0, i))
    if mode in ("q", "k"):
        transposed = mode == "q"
        out_scale = SOFTMAX_SCALE_LOG2 if mode == "q" else 1.0
        kernel = functools.partial(_proj_rope_kernel, transposed=transposed, out_scale=out_scale)
        in_specs = [x_spec, w_spec,
                    pl.BlockSpec((1, HEAD_DIM), lambda b, i, j: (0, 0)),
                    pl.BlockSpec((tm, HEAD_DIM), lambda b, i, j: (i, 0)),
                    pl.BlockSpec((tm, HEAD_DIM), lambda b, i, j: (i, 0))]
        args = (xb, w, gain, cos, sin)
    else:
        transposed = mode == "v"
        kmode = {"v": "transposed", "gate": "sigmoid", "u": "plain"}[mode]
        kernel = functools.partial(_proj_plain_kernel, mode=kmode)
        in_specs = [x_spec, w_spec]
        args = (xb, w)
    if transposed:
        out_shape = jax.ShapeDtypeStruct((B, N // HEAD_DIM, HEAD_DIM, S), out_dtype)
        out_spec = head_out
    else:
        out_shape = jax.ShapeDtypeStruct((B, S, N), out_dtype)
        out_spec = flat_out
    return pl.pallas_call(
        kernel, grid=grid, in_specs=in_specs, out_specs=out_spec, out_shape=out_shape,
        compiler_params=_params(("parallel", "parallel", "arbitrary")),
        name="proj_" + mode,
    )(*args)


def _flash_kernel(qT_ref, k_ref, vT_ref, o_ref, m_ref, l_ref, acc_ref, s_ref, p_ref, mx_ref,
                  *, tk, ts, n_chunks):
    G = qT_ref.shape[0]
    tq = qT_ref.shape[2]
    nt = tk // ts

    def sub8(x):
        return x.reshape(ts // 8, 8, tq)

    def qk_tile(off, g, t):
        kt = k_ref[pl.ds(off + t * ts, ts), :]
        s = jnp.dot(kt, qT_ref[g], preferred_element_type=F32)
        s_ref[g, t * ts:(t + 1) * ts, :] = s
        return jnp.max(sub8(s), axis=0)

    def exp_tile(g, t, m_new):
        p = jnp.exp2(s_ref[g, t * ts:(t + 1) * ts, :] - m_new)
        p_ref[g, t * ts:(t + 1) * ts, :] = p.astype(BF16)
        return jnp.sum(sub8(p), axis=0)

    def chunk_step(c, has_next):
        off = pl.multiple_of(c * tk, tk)
        off_next = pl.multiple_of((c + 1) * tk, tk)
        for g in range(G):
            m_old = m_ref[g]
            m_new = jnp.maximum(m_old, jnp.max(mx_ref[g], axis=0, keepdims=True))
            alpha = jnp.exp2(m_old - m_new)
            m_ref[g] = m_new
            lsum = None
            mx = None
            for t in range(nt):
                ls = exp_tile(g, t, m_new)
                lsum = ls if lsum is None else lsum + ls
                if has_next:
                    tile_mx = qk_tile(off_next, g, t)
                    mx = tile_mx if mx is None else jnp.maximum(mx, tile_mx)
            l_ref[g] = alpha * l_ref[g] + lsum
            acc_ref[g] = alpha * acc_ref[g] + jnp.dot(
                vT_ref[:, pl.ds(off, tk)], p_ref[g], preferred_element_type=F32)
            if has_next:
                mx_ref[g] = mx

    m_ref[...] = jnp.full(m_ref.shape, NEG_BIG, F32)
    l_ref[...] = jnp.zeros(l_ref.shape, F32)
    acc_ref[...] = jnp.zeros(acc_ref.shape, F32)
    for g in range(G):
        mx = None
        for t in range(nt):
            tile_mx = qk_tile(0, g, t)
            mx = tile_mx if mx is None else jnp.maximum(mx, tile_mx)
        mx_ref[g] = mx

    def body(c, carry):
        chunk_step(c, True)
        return carry

    lax.fori_loop(0, n_chunks - 1, body, 0)
    chunk_step(n_chunks - 1, False)

    for g in range(G):
        o = acc_ref[g] / jnp.sum(l_ref[g], axis=0, keepdims=True)
        o_ref[:, g * HEAD_DIM:(g + 1) * HEAD_DIM] = o.T.astype(o_ref.dtype)


def _flash(qT, k, vT, *, tq_pref=256, tk_pref=512, ts_pref=128):
    B, H, hd, S = qT.shape
    G = H // N_KV_HEADS
    tq = _tile(S, tq_pref)
    tk = _tile(S, tk_pref)
    ts = _tile(tk, ts_pref)
    kernel = functools.partial(_flash_kernel, tk=tk, ts=ts, n_chunks=S // tk)
    return pl.pallas_call(
        kernel,
        grid=(B, N_KV_HEADS, S // tq),
        in_specs=[
            pl.BlockSpec((None, G, hd, tq), lambda b, h, i: (b, h, 0, i)),
            pl.BlockSpec((None, S, hd), lambda b, h, i: (b, 0, h)),
            pl.BlockSpec((None, None, hd, S), lambda b, h, i: (b, h, 0, 0)),
        ],
        out_specs=pl.BlockSpec((None, tq, G * hd), lambda b, h, i: (b, i, h)),
        out_shape=jax.ShapeDtypeStruct((B, S, H * hd), F32),
        scratch_shapes=[
            pltpu.VMEM((G, 1, tq), F32),
            pltpu.VMEM((G, 8, tq), F32),
            pltpu.VMEM((G, hd, tq), F32),
            pltpu.VMEM((G, tk, tq), F32),
            pltpu.VMEM((G, tk, tq), BF16),
            pltpu.VMEM((G, 8, tq), F32),
        ],
        compiler_params=_params(("parallel", "parallel", "arbitrary")),
        name="flash_attn",
    )(qT, k, vT)


def _layer_norm(z, g, b):
    mu = jnp.mean(z, axis=-1, keepdims=True)
    zc = z - mu
    var = jnp.mean(zc * zc, axis=-1, keepdims=True)
    return zc * lax.rsqrt(var + LN_EPS) * g + b


def _mix_kernel(a_ref, gt_ref, u_ref, up_ref, un_ref, x_ref, wp_ref, ps_ref, wo_ref,
                g_ref, b_ref, o32_ref, o16_ref, pad_ref, *, alpha, seq):
    tm = u_ref.shape[0]
    d_model = a_ref.shape[1]
    n_groups = len(POOL_WINDOWS)
    gin = u_ref.shape[1] // n_groups
    gout = d_model // n_groups
    i = pl.program_id(1)
    nblk = pl.num_programs(1)
    pad_ref[0:POOL_HALO, :] = jnp.where(i > 0, up_ref[...], 0.0)
    pad_ref[POOL_HALO:POOL_HALO + tm, :] = u_ref[...]
    pad_ref[POOL_HALO + tm:, :] = jnp.where(i < nblk - 1, un_ref[...], 0.0)

    t = i * tm + lax.broadcasted_iota(jnp.int32, (tm, 1), 0)
    outs = []
    for g, w in enumerate(POOL_WINDOWS):
        lo_off = -(w // 2)
        hi_off = w - 1 - w // 2
        c0 = g * gin
        tot = None
        for d in range(lo_off, hi_off + 1):
            v = pad_ref[POOL_HALO + d:POOL_HALO + d + tm, c0:c0 + gin]
            tot = v if tot is None else tot + v
        lo = jnp.maximum(t + lo_off, 0)
        hi = jnp.minimum(t + hi_off, seq - 1)
        cnt = (hi - lo + 1).astype(F32)
        pooled = tot / cnt - u_ref[:, c0:c0 + gin]
        outs.append(jnp.dot(pooled.astype(BF16), wp_ref[g], preferred_element_type=F32))
    p = jnp.concatenate(outs, axis=-1) * ps_ref[...]
    m = gt_ref[:, :d_model] * a_ref[...] + gt_ref[:, d_model:] * p
    y = jnp.dot(m.astype(BF16), wo_ref[...], preferred_element_type=F32)
    out = _layer_norm(alpha * x_ref[...] + y, g_ref[...], b_ref[...])
    o32_ref[...] = out
    o16_ref[...] = out.astype(BF16)


def _mix(a, gates, u, x, w_pool, pool_scale, w_o, ln_g, ln_b, *, alpha, tm_pref=256):
    B, S, D = x.shape
    U = u.shape[-1]
    tm = _tile(S, tm_pref)
    hb = tm // POOL_HALO
    n_halo_blocks = S // POOL_HALO
    kernel = functools.partial(_mix_kernel, alpha=alpha, seq=S)
    row = lambda b, i: (b, i, 0)
    const2 = lambda b, i: (0, 0)
    return pl.pallas_call(
        kernel,
        grid=(B, S // tm),
        in_specs=[
            pl.BlockSpec((None, tm, D), row),
            pl.BlockSpec((None, tm, 2 * D), row),
            pl.BlockSpec((None, tm, U), row),
            pl.BlockSpec((None, POOL_HALO, U),
                         lambda b, i: (b, jnp.maximum(i * hb - 1, 0), 0)),
            pl.BlockSpec((None, POOL_HALO, U),
                         lambda b, i: (b, jnp.minimum((i + 1) * hb, n_halo_blocks - 1), 0)),
            pl.BlockSpec((None, tm, D), row),
            pl.BlockSpec(w_pool.shape, lambda b, i: (0, 0, 0)),
            pl.BlockSpec((1, D), const2),
            pl.BlockSpec((D, D), const2),
            pl.BlockSpec((1, D), const2),
            pl.BlockSpec((1, D), const2),
        ],
        out_specs=[pl.BlockSpec((None, tm, D), row), pl.BlockSpec((None, tm, D), row)],
        out_shape=[jax.ShapeDtypeStruct((B, S, D), F32), jax.ShapeDtypeStruct((B, S, D), BF16)],
        scratch_shapes=[pltpu.VMEM((tm + 2 * POOL_HALO, U), F32)],
        compiler_params=_params(("parallel", "arbitrary")),
        name="mix_wo_ln",
    )(a, gates, u, u, u, x, w_pool, pool_scale, w_o, ln_g, ln_b)


def _mlp_kernel(xb_ref, x_ref, wu_ref, wd_ref, g_ref, b_ref, o32_ref, o16_ref, acc_ref, *, alpha):
    f = pl.program_id(2)
    h = jnp.dot(xb_ref[...], wu_ref[...], preferred_element_type=F32)
    h = jnp.square(jnp.maximum(h, 0.0))
    contrib = jnp.dot(h.astype(BF16), wd_ref[...], preferred_element_type=F32)

    @pl.when(f == 0)
    def _():
        acc_ref[...] = contrib

    @pl.when(f > 0)
    def _():
        acc_ref[...] += contrib

    @pl.when(f == pl.num_programs(2) - 1)
    def _():
        out = _layer_norm(alpha * x_ref[...] + acc_ref[...], g_ref[...], b_ref[...])
        o32_ref[...] = out
        o16_ref[...] = out.astype(BF16)


def _mlp(xb, x, w_up, w_down, ln_g, ln_b, *, alpha, tm_pref=512, tf_pref=512):
    B, S, D = x.shape
    F = w_up.shape[1]
    tm = _tile(S, tm_pref)
    tf = _tile(F, tf_pref)
    kernel = functools.partial(_mlp_kernel, alpha=alpha)
    row = lambda b, i, f: (b, i, 0)
    const2 = lambda b, i, f: (0, 0)
    return pl.pallas_call(
        kernel,
        grid=(B, S // tm, F // tf),
        in_specs=[
            pl.BlockSpec((None, tm, D), row),
            pl.BlockSpec((None, tm, D), row),
            pl.BlockSpec((D, tf), lambda b, i, f: (0, f)),
            pl.BlockSpec((tf, D), lambda b, i, f: (f, 0)),
            pl.BlockSpec((1, D), const2),
            pl.BlockSpec((1, D), const2),
        ],
        out_specs=[pl.BlockSpec((None, tm, D), row), pl.BlockSpec((None, tm, D), row)],
        out_shape=[jax.ShapeDtypeStruct((B, S, D), F32), jax.ShapeDtypeStruct((B, S, D), BF16)],
        scratch_shapes=[pltpu.VMEM((tm, D), F32)],
        compiler_params=_params(("parallel", "parallel", "arbitrary")),
        name="mlp_ln",
    )(xb, x, w_up, w_down, ln_g, ln_b)


def _trunk(x, w):
    B, S, D = x.shape
    depth = w["w_in"].shape[0]
    alpha = (2 * depth) ** 0.25
    attn_w = D
    kv_w = N_KV_HEADS * HEAD_DIM
    pool_w = D // 2
    o1 = attn_w
    o2 = o1 + kv_w
    o3 = o2 + kv_w
    o4 = o3 + pool_w
    cos, sin = _rope_tables(S)
    xb = x.astype(BF16)
    for l in range(depth):
        w_in = w["w_in"][l]
        qT = _proj(xb, w_in[:, :o1], mode="q", gain=w["q_norm"][l], cos=cos, sin=sin, out_dtype=BF16)
        k = _proj(xb, w_in[:, o1:o2], mode="k", gain=w["k_norm"][l], cos=cos, sin=sin, out_dtype=BF16)
        vT = _proj(xb, w_in[:, o2:o3], mode="v", out_dtype=BF16)
        u = _proj(xb, w_in[:, o3:o4], mode="u")
        gates = _proj(xb, w_in[:, o4:], mode="gate")
        a = _flash(qT, k, vT)
        x, xb = _mix(a, gates, u, x, w["w_pool"][l], w["pool_scale"][l], w["w_o"][l],
                     w["ln1_g"][l], w["ln1_b"][l], alpha=alpha)
        x, xb = _mlp(xb, x, w["w_up"][l], w["w_down"][l], w["ln2_g"][l], w["ln2_b"][l], alpha=alpha)
    return x


def kernel(x_prompt, x_sample, w_in, q_norm, k_norm, w_pool, pool_scale, w_o,
           ln1_g, ln1_b, w_up, w_down, ln2_g, ln2_b):
    depth = w_in.shape[0]
    row = lambda v: v.reshape(depth, 1, v.shape[-1])
    w = {
        "w_in": w_in.astype(BF16),
        "q_norm": row(q_norm), "k_norm": row(k_norm),
        "w_pool": w_pool.astype(BF16), "pool_scale": row(pool_scale),
        "w_o": w_o.astype(BF16),
        "ln1_g": row(ln1_g), "ln1_b": row(ln1_b),
        "w_up": w_up.astype(BF16), "w_down": w_down.astype(BF16),
        "ln2_g": row(ln2_g), "ln2_b": row(ln2_b),
    }
    return (_trunk(x_prompt, w), _trunk(x_sample, w))
```

```python
import functools

import jax
import jax.numpy as jnp
from jax import lax
from jax.experimental import pallas as pl
from jax.experimental.pallas import tpu as pltpu

F32 = jnp.float32
BF16 = jnp.bfloat16

HEAD_DIM = 128
N_KV_HEADS = 4
GRID_W = 64
ROPE_THETA = 10000.0
POOL_WINDOWS = (2, 4, 8, 16)
POOL_HALO = 8
LN_EPS = 1e-5
QK_EPS = 1e-6
LOG2E = 1.4426950408889634
SOFTMAX_SCALE_LOG2 = HEAD_DIM ** -0.5 * LOG2E
NEG_BIG = -1e30
ROPE_ROW_BLOCK = 256

VMEM_LIMIT_BYTES = 56 * 1024 * 1024


def _tile(n, pref):
    t = min(n, pref)
    assert n % t == 0, (n, pref)
    return t


def _params(sem):
    return pltpu.CompilerParams(dimension_semantics=sem,
                                vmem_limit_bytes=VMEM_LIMIT_BYTES)


def _permute_head_lanes(a):
    lead = a.shape[:-1]
    n = a.shape[-1] // HEAD_DIM
    a = a.reshape(lead + (n, 2, 2, HEAD_DIM // 4))
    a = jnp.swapaxes(a, -3, -2)
    return a.reshape(lead + (n * HEAD_DIM,))


def _rope_tables(seq):
    t = jnp.arange(seq)
    row = (t // GRID_W).astype(F32)
    col = (t % GRID_W).astype(F32)
    axis_dim = HEAD_DIM // 2
    inv_freq = ROPE_THETA ** (-jnp.arange(0, axis_dim, 2, dtype=F32) / axis_dim)
    ar = row[:, None] * inv_freq[None, :]
    ac = col[:, None] * inv_freq[None, :]
    cos = jnp.concatenate([jnp.cos(ar), jnp.cos(ac), jnp.cos(ar), jnp.cos(ac)], axis=-1)
    sin = jnp.concatenate([-jnp.sin(ar), -jnp.sin(ac), jnp.sin(ar), jnp.sin(ac)], axis=-1)
    return cos, sin


def _proj_rope_kernel(x_ref, w_ref, g_ref, cos_ref, sin_ref, o_ref, acc_ref, *, rb):
    acc_ref[...] = jnp.dot(x_ref[...], w_ref[...], preferred_element_type=F32)
    tm, tn = acc_ref.shape
    gain = g_ref[...]

    def row_block(r, carry):
        rows = pl.ds(pl.multiple_of(r * rb, rb), rb)
        cos = cos_ref[rows, :]
        sin = sin_ref[rows, :]
        for h in range(tn // HEAD_DIM):
            cols = slice(h * HEAD_DIM, (h + 1) * HEAD_DIM)
            xh = acc_ref[rows, cols]
            ms = jnp.mean(xh * xh, axis=-1, keepdims=True)
            y = xh * lax.rsqrt(ms + QK_EPS) * gain
            rot = y * cos + pltpu.roll(y, HEAD_DIM // 2, 1) * sin
            o_ref[rows, cols] = rot.astype(o_ref.dtype)
        return carry

    lax.fori_loop(0, tm // rb, row_block, 0)


def _proj_plain_kernel(x_ref, w_ref, o_ref, *, mode):
    acc = jnp.dot(x_ref[...], w_ref[...], preferred_element_type=F32)
    if mode == "transposed":
        for h in range(acc.shape[1] // HEAD_DIM):
            o_ref[h] = acc[:, h * HEAD_DIM:(h + 1) * HEAD_DIM].T.astype(o_ref.dtype)
    elif mode == "sigmoid":
        o_ref[...] = (1.0 / (1.0 + jnp.exp(-acc))).astype(o_ref.dtype)
    else:
        o_ref[...] = acc.astype(o_ref.dtype)


def _proj(xb, w, *, mode, gains=None, cos=None, sin=None, out_dtype=F32, tm_pref=1024, tn=512):
    B, S, D = xb.shape
    N = w.shape[1]
    tm = _tile(S, tm_pref)
    tn = _tile(N, tn)
    grid = (B, S // tm, N // tn)
    hb = tn // HEAD_DIM
    x_spec = pl.BlockSpec((None, tm, D), lambda b, i, j: (b, i, 0))
    w_spec = pl.BlockSpec((D, tn), lambda b, i, j: (0, j))
    flat_out = pl.BlockSpec((None, tm, tn), lambda b, i, j: (b, i, j))
    head_out = pl.BlockSpec((None, hb, HEAD_DIM, tm), lambda b, i, j: (b, j, 0, i))
    if mode == "qk":
        kernel = functools.partial(_proj_rope_kernel, rb=_tile(tm, ROPE_ROW_BLOCK))
        in_specs = [x_spec, w_spec,
                    pl.BlockSpec((None, 1, HEAD_DIM), lambda b, i, j: (j, 0, 0)),
                    pl.BlockSpec((tm, HEAD_DIM), lambda b, i, j: (i, 0)),
                    pl.BlockSpec((tm, HEAD_DIM), lambda b, i, j: (i, 0))]
        args = (xb, w, gains, cos, sin)
        scratch = [pltpu.VMEM((tm, tn), F32)]
    else:
        kmode = {"v": "transposed", "gate": "sigmoid", "u": "plain"}[mode]
        kernel = functools.partial(_proj_plain_kernel, mode=kmode)
        in_specs = [x_spec, w_spec]
        args = (xb, w)
        scratch = []
    if mode == "v":
        out_shape = jax.ShapeDtypeStruct((B, N // HEAD_DIM, HEAD_DIM, S), out_dtype)
        out_spec = head_out
    else:
        out_shape = jax.ShapeDtypeStruct((B, S, N), out_dtype)
        out_spec = flat_out
    return pl.pallas_call(
        kernel, grid=grid, in_specs=in_specs, out_specs=out_spec, out_shape=out_shape,
        scratch_shapes=scratch,
        compiler_params=_params(("parallel", "parallel", "arbitrary")),
        name="proj_" + mode,
    )(*args)


def _flash_kernel(q_ref, qn_ref, k_ref, vT_ref, o_ref,
                  qT_ref, qTn_ref, m_ref, l_ref, acc_ref, s_ref, p_ref, mx_ref,
                  *, tk, ts, n_chunks):
    G = qT_ref.shape[0]
    tq = qT_ref.shape[2]
    nt = tk // ts

    def transpose_heads(src_ref, dst_ref):
        for g in range(G):
            qh = src_ref[:, g * HEAD_DIM:(g + 1) * HEAD_DIM].astype(F32)
            dst_ref[g] = qh.T.astype(BF16)

    def sub8(x):
        return x.reshape(ts // 8, 8, tq)

    def qk_tile(qt_ref, off, g, t):
        kt = k_ref[pl.ds(off + t * ts, ts), :]
        s = jnp.dot(kt, qt_ref[g], preferred_element_type=F32)
        s_ref[g, t * ts:(t + 1) * ts, :] = s
        return jnp.max(sub8(s), axis=0)

    def exp_tile(g, t, m_new):
        p = jnp.exp2(s_ref[g, t * ts:(t + 1) * ts, :] - m_new)
        p_ref[g, t * ts:(t + 1) * ts, :] = p.astype(BF16)
        return jnp.sum(sub8(p), axis=0)

    def chunk_step(off, qt_next_ref, off_next):
        for g in range(G):
            m_old = m_ref[g]
            m_new = jnp.maximum(m_old, jnp.max(mx_ref[g], axis=0, keepdims=True))
            alpha = jnp.exp2(m_old - m_new)
            m_ref[g] = m_new
            lsum = None
            mx = None
            for t in range(nt):
                ls = exp_tile(g, t, m_new)
                lsum = ls if lsum is None else lsum + ls
                tile_mx = qk_tile(qt_next_ref, off_next, g, t)
                mx = tile_mx if mx is None else jnp.maximum(mx, tile_mx)
            l_ref[g] = alpha * l_ref[g] + lsum
            acc_ref[g] = alpha * acc_ref[g] + jnp.dot(
                vT_ref[:, pl.ds(off, tk)], p_ref[g], preferred_element_type=F32)
            mx_ref[g] = mx

    m_ref[...] = jnp.full(m_ref.shape, NEG_BIG, F32)
    l_ref[...] = jnp.zeros(l_ref.shape, F32)
    acc_ref[...] = jnp.zeros(acc_ref.shape, F32)

    @pl.when(pl.program_id(2) == 0)
    def _():
        transpose_heads(q_ref, qT_ref)
        for g in range(G):
            mx = None
            for t in range(nt):
                tile_mx = qk_tile(qT_ref, 0, g, t)
                mx = tile_mx if mx is None else jnp.maximum(mx, tile_mx)
            mx_ref[g] = mx

    transpose_heads(qn_ref, qTn_ref)

    def body(c, carry):
        off = pl.multiple_of(c * tk, tk)
        chunk_step(off, qT_ref, pl.multiple_of(off + tk, tk))
        return carry

    lax.fori_loop(0, n_chunks - 1, body, 0)
    chunk_step((n_chunks - 1) * tk, qTn_ref, 0)

    for g in range(G):
        o = acc_ref[g] / jnp.sum(l_ref[g], axis=0, keepdims=True)
        o_ref[:, g * HEAD_DIM:(g + 1) * HEAD_DIM] = o.T.astype(o_ref.dtype)
    qT_ref[...] = qTn_ref[...]


def _flash(qk, vT, *, n_heads, tq_pref=256, tk_pref=1024, ts_pref=128):
    B, KV, hd, S = vT.shape
    G = n_heads // KV
    tq = _tile(S, tq_pref)
    tk = _tile(S, tk_pref)
    ts = _tile(tk, ts_pref)
    nq = S // tq
    kernel = functools.partial(_flash_kernel, tk=tk, ts=ts, n_chunks=S // tk)
    return pl.pallas_call(
        kernel,
        grid=(B, KV, nq),
        in_specs=[
            pl.BlockSpec((None, tq, G * hd), lambda b, h, i: (b, i, h)),
            pl.BlockSpec((None, tq, G * hd), lambda b, h, i: (b, jnp.minimum(i + 1, nq - 1), h)),
            pl.BlockSpec((None, S, hd), lambda b, h, i: (b, 0, n_heads + h)),
            pl.BlockSpec((None, None, hd, S), lambda b, h, i: (b, h, 0, 0)),
        ],
        out_specs=pl.BlockSpec((None, tq, G * hd), lambda b, h, i: (b, i, h)),
        out_shape=jax.ShapeDtypeStruct((B, S, n_heads * hd), F32),
        scratch_shapes=[
            pltpu.VMEM((G, hd, tq), BF16),
            pltpu.VMEM((G, hd, tq), BF16),
            pltpu.VMEM((G, 1, tq), F32),
            pltpu.VMEM((G, 8, tq), F32),
            pltpu.VMEM((G, hd, tq), F32),
            pltpu.VMEM((G, tk, tq), F32),
            pltpu.VMEM((G, tk, tq), BF16),
            pltpu.VMEM((G, 8, tq), F32),
        ],
        compiler_params=_params(("parallel", "parallel", "arbitrary")),
        name="flash_attn",
    )(qk, qk, qk, vT)


def _layer_norm(z, g, b):
    mu = jnp.mean(z, axis=-1, keepdims=True)
    zc = z - mu
    var = jnp.mean(zc * zc, axis=-1, keepdims=True)
    return zc * lax.rsqrt(var + LN_EPS) * g + b


def _mix_kernel(a_ref, gt_ref, u_ref, up_ref, un_ref, x_ref, wp_ref, ps_ref, wo_ref,
                g_ref, b_ref, o32_ref, o16_ref, pad_ref, *, alpha, seq):
    tm = u_ref.shape[0]
    d_model = a_ref.shape[1]
    n_groups = len(POOL_WINDOWS)
    gin = u_ref.shape[1] // n_groups
    i = pl.program_id(1)
    nblk = pl.num_programs(1)
    pad_ref[0:POOL_HALO, :] = jnp.where(i > 0, up_ref[...], 0.0)
    pad_ref[POOL_HALO:POOL_HALO + tm, :] = u_ref[...]
    pad_ref[POOL_HALO + tm:, :] = jnp.where(i < nblk - 1, un_ref[...], 0.0)

    t = i * tm + lax.broadcasted_iota(jnp.int32, (tm, 1), 0)
    outs = []
    for g, w in enumerate(POOL_WINDOWS):
        lo_off = -(w // 2)
        hi_off = w - 1 - w // 2
        c0 = g * gin
        tot = None
        for d in range(lo_off, hi_off + 1):
            v = pad_ref[POOL_HALO + d:POOL_HALO + d + tm, c0:c0 + gin]
            tot = v if tot is None else tot + v
        lo = jnp.maximum(t + lo_off, 0)
        hi = jnp.minimum(t + hi_off, seq - 1)
        cnt = (hi - lo + 1).astype(F32)
        pooled = tot / cnt - u_ref[:, c0:c0 + gin]
        outs.append(jnp.dot(pooled.astype(BF16), wp_ref[g], preferred_element_type=F32))
    p = jnp.concatenate(outs, axis=-1) * ps_ref[...]
    m = gt_ref[:, :d_model] * a_ref[...] + gt_ref[:, d_model:] * p
    y = jnp.dot(m.astype(BF16), wo_ref[...], preferred_element_type=F32)
    out = _layer_norm(alpha * x_ref[...] + y, g_ref[...], b_ref[...])
    o32_ref[...] = out
    o16_ref[...] = out.astype(BF16)


def _mix(a, gates, u, x, w_pool, pool_scale, w_o, ln_g, ln_b, *, alpha, tm_pref=256):
    B, S, D = x.shape
    U = u.shape[-1]
    tm = _tile(S, tm_pref)
    hb = tm // POOL_HALO
    n_halo_blocks = S // POOL_HALO
    kernel = functools.partial(_mix_kernel, alpha=alpha, seq=S)
    row = lambda b, i: (b, i, 0)
    const2 = lambda b, i: (0, 0)
    return pl.pallas_call(
        kernel,
        grid=(B, S // tm),
        in_specs=[
            pl.BlockSpec((None, tm, D), row),
            pl.BlockSpec((None, tm, 2 * D), row),
            pl.BlockSpec((None, tm, U), row),
            pl.BlockSpec((None, POOL_HALO, U),
                         lambda b, i: (b, jnp.maximum(i * hb - 1, 0), 0)),
            pl.BlockSpec((None, POOL_HALO, U),
                         lambda b, i: (b, jnp.minimum((i + 1) * hb, n_halo_blocks - 1), 0)),
            pl.BlockSpec((None, tm, D), row),
            pl.BlockSpec(w_pool.shape, lambda b, i: (0, 0, 0)),
            pl.BlockSpec((1, D), const2),
            pl.BlockSpec((D, D), const2),
            pl.BlockSpec((1, D), const2),
            pl.BlockSpec((1, D), const2),
        ],
        out_specs=[pl.BlockSpec((None, tm, D), row), pl.BlockSpec((None, tm, D), row)],
        out_shape=[jax.ShapeDtypeStruct((B, S, D), F32), jax.ShapeDtypeStruct((B, S, D), BF16)],
        scratch_shapes=[pltpu.VMEM((tm + 2 * POOL_HALO, U), F32)],
        compiler_params=_params(("parallel", "arbitrary")),
        name="mix_wo_ln",
    )(a, gates, u, u, u, x, w_pool, pool_scale, w_o, ln_g, ln_b)


def _mlp_kernel(xb_ref, x_ref, wu_ref, wd_ref, g_ref, b_ref, o32_ref, o16_ref, acc_ref, *, alpha):
    f = pl.program_id(2)

    @pl.when(f == 0)
    def _():
        acc_ref[...] = jnp.zeros(acc_ref.shape, F32)

    h = jnp.dot(xb_ref[...], wu_ref[...], preferred_element_type=F32)
    h = jnp.square(jnp.maximum(h, 0.0))
    acc_ref[...] += jnp.dot(h.astype(BF16), wd_ref[...], preferred_element_type=F32)

    @pl.when(f == pl.num_programs(2) - 1)
    def _():
        out = _layer_norm(alpha * x_ref[...] + acc_ref[...], g_ref[...], b_ref[...])
        o32_ref[...] = out
        o16_ref[...] = out.astype(BF16)


def _mlp(xb, x, w_up, w_down, ln_g, ln_b, *, alpha, tm_pref=512, tf_pref=512):
    B, S, D = x.shape
    F = w_up.shape[1]
    tm = _tile(S, tm_pref)
    tf = _tile(F, tf_pref)
    kernel = functools.partial(_mlp_kernel, alpha=alpha)
    row = lambda b, i, f: (b, i, 0)
    const2 = lambda b, i, f: (0, 0)
    return pl.pallas_call(
        kernel,
        grid=(B, S // tm, F // tf),
        in_specs=[
            pl.BlockSpec((None, tm, D), row),
            pl.BlockSpec((None, tm, D), row),
            pl.BlockSpec((D, tf), lambda b, i, f: (0, f)),
            pl.BlockSpec((tf, D), lambda b, i, f: (f, 0)),
            pl.BlockSpec((1, D), const2),
            pl.BlockSpec((1, D), const2),
        ],
        out_specs=[pl.BlockSpec((None, tm, D), row), pl.BlockSpec((None, tm, D), row)],
        out_shape=[jax.ShapeDtypeStruct((B, S, D), F32), jax.ShapeDtypeStruct((B, S, D), BF16)],
        scratch_shapes=[pltpu.VMEM((tm, D), F32)],
        compiler_params=_params(("parallel", "parallel", "arbitrary")),
        name="mlp_ln",
    )(xb, x, w_up, w_down, ln_g, ln_b)


def _trunk(x, w):
    B, S, D = x.shape
    depth = w["w_qk"].shape[0]
    alpha = (2 * depth) ** 0.25
    n_heads = D // HEAD_DIM
    cos, sin = _rope_tables(S)
    xb = x.astype(BF16)
    for l in range(depth):
        qk = _proj(xb, w["w_qk"][l], mode="qk", gains=w["qk_gains"][l], cos=cos, sin=sin,
                   out_dtype=BF16)
        vT = _proj(xb, w["w_v"][l], mode="v", out_dtype=BF16)
        u = _proj(xb, w["w_u"][l], mode="u")
        gates = _proj(xb, w["w_gate"][l], mode="gate")
        a = _flash(qk, vT, n_heads=n_heads)
        x, xb = _mix(a, gates, u, x, w["w_pool"][l], w["pool_scale"][l], w["w_o"][l],
                     w["ln1_g"][l], w["ln1_b"][l], alpha=alpha)
        x, xb = _mlp(xb, x, w["w_up"][l], w["w_down"][l], w["ln2_g"][l], w["ln2_b"][l], alpha=alpha)
    return x


def kernel(x_prompt, x_sample, w_in, q_norm, k_norm, w_pool, pool_scale, w_o,
           ln1_g, ln1_b, w_up, w_down, ln2_g, ln2_b):
    depth, d_model, _ = w_in.shape
    kv_w = N_KV_HEADS * HEAD_DIM
    o1 = d_model
    o2 = o1 + kv_w
    o3 = o2 + kv_w
    o4 = o3 + d_model // 2
    qk_tn = 512
    w_in_b = w_in.astype(BF16)
    gq = _permute_head_lanes(q_norm) * SOFTMAX_SCALE_LOG2
    gk = _permute_head_lanes(k_norm)
    qk_gains = jnp.concatenate(
        [jnp.broadcast_to(gq[:, None, None, :], (depth, o1 // qk_tn, 1, HEAD_DIM)),
         jnp.broadcast_to(gk[:, None, None, :], (depth, kv_w // qk_tn, 1, HEAD_DIM))], axis=1)
    row = lambda v: v.reshape(depth, 1, v.shape[-1])
    w = {
        "w_qk": _permute_head_lanes(w_in_b[:, :, :o2]), "qk_gains": qk_gains,
        "w_v": w_in_b[:, :, o2:o3], "w_u": w_in_b[:, :, o3:o4], "w_gate": w_in_b[:, :, o4:],
        "w_pool": w_pool.astype(BF16), "pool_scale": row(pool_scale),
        "w_o": w_o.astype(BF16),
        "ln1_g": row(ln1_g), "ln1_b": row(ln1_b),
        "w_up": w_up.astype(BF16), "w_down": w_down.astype(BF16),
        "ln2_g": row(ln2_g), "ln2_b": row(ln2_b),
    }
    return (_trunk(x_prompt, w), _trunk(x_sample, w))
```

```python
import functools

import jax
import jax.numpy as jnp
from jax import lax
from jax.experimental import pallas as pl
from jax.experimental.pallas import tpu as pltpu

F32 = jnp.float32
BF16 = jnp.bfloat16

HEAD_DIM = 128
N_KV_HEADS = 4
GRID_W = 64
ROPE_THETA = 10000.0
POOL_WINDOWS = (2, 4, 8, 16)
POOL_HALO = 8
LN_EPS = 1e-5
QK_EPS = 1e-6
LOG2E = 1.4426950408889634
SOFTMAX_SCALE_LOG2 = HEAD_DIM ** -0.5 * LOG2E
NEG_BIG = -1e30
ROPE_ROW_BLOCK = 256
PROJ_TN = 512

VMEM_LIMIT_BYTES = 56 * 1024 * 1024


def _tile(n, pref):
    t = min(n, pref)
    assert n % t == 0, (n, pref)
    return t


def _params(sem):
    return pltpu.CompilerParams(dimension_semantics=sem,
                                vmem_limit_bytes=VMEM_LIMIT_BYTES)


def _permute_head_lanes(a):
    lead = a.shape[:-1]
    n = a.shape[-1] // HEAD_DIM
    a = a.reshape(lead + (n, 2, 2, HEAD_DIM // 4))
    a = jnp.swapaxes(a, -3, -2)
    return a.reshape(lead + (n * HEAD_DIM,))


def _rope_tables(seq):
    t = jnp.arange(seq)
    row = (t // GRID_W).astype(F32)
    col = (t % GRID_W).astype(F32)
    axis_dim = HEAD_DIM // 2
    inv_freq = ROPE_THETA ** (-jnp.arange(0, axis_dim, 2, dtype=F32) / axis_dim)
    ar = row[:, None] * inv_freq[None, :]
    ac = col[:, None] * inv_freq[None, :]
    cos = jnp.concatenate([jnp.cos(ar), jnp.cos(ac), jnp.cos(ar), jnp.cos(ac)], axis=-1)
    sin = jnp.concatenate([-jnp.sin(ar), -jnp.sin(ac), jnp.sin(ar), jnp.sin(ac)], axis=-1)
    return cos, sin


def _proj_kernel(x_ref, w_ref, g_ref, cos_ref, sin_ref, qk_ref, vT_ref, u_ref, gt_ref, acc_ref,
                 *, rb, j_v, j_u, j_gate):
    tm, tn = acc_ref.shape
    j = pl.program_id(2)
    heads = [slice(h * HEAD_DIM, (h + 1) * HEAD_DIM) for h in range(tn // HEAD_DIM)]
    row_blocks = [slice(r * rb, (r + 1) * rb) for r in range(tm // rb)]

    def block_matmul(rows):
        acc_ref[rows, :] = jnp.dot(x_ref[rows, :], w_ref[...], preferred_element_type=F32)

    def interleaved(epilogue):
        block_matmul(row_blocks[0])
        for r, rows in enumerate(row_blocks):
            if r + 1 < len(row_blocks):
                block_matmul(row_blocks[r + 1])
            epilogue(rows)

    def rope_epilogue(rows):
        gain = g_ref[...]
        cos = cos_ref[rows, :]
        sin = sin_ref[rows, :]
        for cols in heads:
            xh = acc_ref[rows, cols]
            ms = jnp.mean(xh * xh, axis=-1, keepdims=True)
            y = xh * lax.rsqrt(ms + QK_EPS) * gain
            rot = y * cos + pltpu.roll(y, HEAD_DIM // 2, 1) * sin
            qk_ref[rows, cols] = rot.astype(qk_ref.dtype)

    def v_epilogue(rows):
        for h, cols in enumerate(heads):
            vT_ref[h, :, rows] = acc_ref[rows, cols].T.astype(vT_ref.dtype)

    def gate_epilogue(rows):
        gt_ref[rows, :] = 1.0 / (1.0 + jnp.exp(-acc_ref[rows, :]))

    pl.when(j < j_v)(lambda: interleaved(rope_epilogue))
    pl.when(j == j_v)(lambda: interleaved(v_epilogue))
    pl.when(j >= j_gate)(lambda: interleaved(gate_epilogue))

    @pl.when((j >= j_u) & (j < j_gate))
    def _():
        u_ref[...] = jnp.dot(x_ref[...], w_ref[...], preferred_element_type=F32)


def _proj(xb, w, gains, cos, sin, *, qk_w, v_w, u_w, tm_pref=1024, tn=512):
    B, S, D = xb.shape
    N = w.shape[1]
    assert v_w == tn and qk_w % tn == 0 and u_w % tn == 0 and N % tn == 0
    tm = _tile(S, tm_pref)
    rb = _tile(tm, ROPE_ROW_BLOCK)
    j_v = qk_w // tn
    j_u = j_v + 1
    j_gate = j_u + u_w // tn
    gate_w = N - j_gate * tn
    kernel = functools.partial(_proj_kernel, rb=rb, j_v=j_v, j_u=j_u, j_gate=j_gate)
    clamp = lambda v, lo, hi: jnp.minimum(jnp.maximum(v, lo), hi)
    return pl.pallas_call(
        kernel,
        grid=(B, S // tm, N // tn),
        in_specs=[
            pl.BlockSpec((None, tm, D), lambda b, i, j: (b, i, 0)),
            pl.BlockSpec((D, tn), lambda b, i, j: (0, j)),
            pl.BlockSpec((None, 1, HEAD_DIM), lambda b, i, j: (jnp.minimum(j, j_v - 1), 0, 0)),
            pl.BlockSpec((tm, HEAD_DIM), lambda b, i, j: (i, 0)),
            pl.BlockSpec((tm, HEAD_DIM), lambda b, i, j: (i, 0)),
        ],
        out_specs=[
            pl.BlockSpec((None, tm, tn), lambda b, i, j: (b, i, jnp.minimum(j, j_v - 1))),
            pl.BlockSpec((None, tn // HEAD_DIM, HEAD_DIM, tm), lambda b, i, j: (b, 0, 0, i)),
            pl.BlockSpec((None, tm, tn), lambda b, i, j: (b, i, clamp(j - j_u, 0, j_gate - j_u - 1))),
            pl.BlockSpec((None, tm, tn), lambda b, i, j: (b, i, jnp.maximum(j - j_gate, 0))),
        ],
        out_shape=[
            jax.ShapeDtypeStruct((B, S, qk_w), BF16),
            jax.ShapeDtypeStruct((B, v_w // HEAD_DIM, HEAD_DIM, S), BF16),
            jax.ShapeDtypeStruct((B, S, u_w), F32),
            jax.ShapeDtypeStruct((B, S, gate_w), F32),
        ],
        scratch_shapes=[pltpu.VMEM((tm, tn), F32)],
        compiler_params=_params(("parallel", "parallel", "arbitrary")),
        name="proj",
    )(xb, w, gains, cos, sin)


def _flash_kernel(q_ref, qn_ref, k_ref, vT_ref, o_ref,
                  qT_ref, qTn_ref, m_ref, l_ref, acc_ref, s_ref, p_ref, mx_ref,
                  *, tk, ts, n_chunks):
    G = qT_ref.shape[0]
    tq = qT_ref.shape[2]
    nt = tk // ts

    def transpose_heads(src_ref, dst_ref):
        for g in range(G):
            qh = src_ref[:, g * HEAD_DIM:(g + 1) * HEAD_DIM].astype(F32)
            dst_ref[g] = qh.T.astype(BF16)

    def sub8(x):
        return x.reshape(ts // 8, 8, tq)

    def qk_tile(qt_ref, off, g, t):
        kt = k_ref[pl.ds(off + t * ts, ts), :]
        s = jnp.dot(kt, qt_ref[g], preferred_element_type=F32)
        s_ref[g, t * ts:(t + 1) * ts, :] = s
        return jnp.max(sub8(s), axis=0)

    def exp_tile(g, t, m_new):
        p = jnp.exp2(s_ref[g, t * ts:(t + 1) * ts, :] - m_new)
        p_ref[g, t * ts:(t + 1) * ts, :] = p.astype(BF16)
        return jnp.sum(sub8(p), axis=0)

    def chunk_step(off, qt_next_ref, off_next):
        for g in range(G):
            m_old = m_ref[g]
            m_new = jnp.maximum(m_old, jnp.max(mx_ref[g], axis=0, keepdims=True))
            alpha = jnp.exp2(m_old - m_new)
            m_ref[g] = m_new
            lsum = None
            mx = None
            for t in range(nt):
                ls = exp_tile(g, t, m_new)
                lsum = ls if lsum is None else lsum + ls
                tile_mx = qk_tile(qt_next_ref, off_next, g, t)
                mx = tile_mx if mx is None else jnp.maximum(mx, tile_mx)
            l_ref[g] = alpha * l_ref[g] + lsum
            acc_ref[g] = alpha * acc_ref[g] + jnp.dot(
                vT_ref[:, pl.ds(off, tk)], p_ref[g], preferred_element_type=F32)
            mx_ref[g] = mx

    m_ref[...] = jnp.full(m_ref.shape, NEG_BIG, F32)
    l_ref[...] = jnp.zeros(l_ref.shape, F32)
    acc_ref[...] = jnp.zeros(acc_ref.shape, F32)

    @pl.when(pl.program_id(2) == 0)
    def _():
        transpose_heads(q_ref, qT_ref)
        for g in range(G):
            mx = None
            for t in range(nt):
                tile_mx = qk_tile(qT_ref, 0, g, t)
                mx = tile_mx if mx is None else jnp.maximum(mx, tile_mx)
            mx_ref[g] = mx

    transpose_heads(qn_ref, qTn_ref)

    def body(c, carry):
        off = pl.multiple_of(c * tk, tk)
        chunk_step(off, qT_ref, pl.multiple_of(off + tk, tk))
        return carry

    lax.fori_loop(0, n_chunks - 1, body, 0)
    chunk_step((n_chunks - 1) * tk, qTn_ref, 0)

    for g in range(G):
        o = acc_ref[g] / jnp.sum(l_ref[g], axis=0, keepdims=True)
        o_ref[:, g * HEAD_DIM:(g + 1) * HEAD_DIM] = o.T.astype(o_ref.dtype)
    qT_ref[...] = qTn_ref[...]


def _flash(qk, vT, *, n_heads, tq_pref=256, tk_pref=1024, ts_pref=128):
    B, KV, hd, S = vT.shape
    G = n_heads // KV
    tq = _tile(S, tq_pref)
    tk = _tile(S, tk_pref)
    ts = _tile(tk, ts_pref)
    nq = S // tq
    kernel = functools.partial(_flash_kernel, tk=tk, ts=ts, n_chunks=S // tk)
    return pl.pallas_call(
        kernel,
        grid=(B, KV, nq),
        in_specs=[
            pl.BlockSpec((None, tq, G * hd), lambda b, h, i: (b, i, h)),
            pl.BlockSpec((None, tq, G * hd), lambda b, h, i: (b, jnp.minimum(i + 1, nq - 1), h)),
            pl.BlockSpec((None, S, hd), lambda b, h, i: (b, 0, n_heads + h)),
            pl.BlockSpec((None, None, hd, S), lambda b, h, i: (b, h, 0, 0)),
        ],
        out_specs=pl.BlockSpec((None, tq, G * hd), lambda b, h, i: (b, i, h)),
        out_shape=jax.ShapeDtypeStruct((B, S, n_heads * hd), F32),
        scratch_shapes=[
            pltpu.VMEM((G, hd, tq), BF16),
            pltpu.VMEM((G, hd, tq), BF16),
            pltpu.VMEM((G, 1, tq), F32),
            pltpu.VMEM((G, 8, tq), F32),
            pltpu.VMEM((G, hd, tq), F32),
            pltpu.VMEM((G, tk, tq), F32),
            pltpu.VMEM((G, tk, tq), BF16),
            pltpu.VMEM((G, 8, tq), F32),
        ],
        compiler_params=_params(("parallel", "parallel", "arbitrary")),
        name="flash_attn",
    )(qk, qk, qk, vT)


def _layer_norm(z, g, b):
    mu = jnp.mean(z, axis=-1, keepdims=True)
    zc = z - mu
    var = jnp.mean(zc * zc, axis=-1, keepdims=True)
    return zc * lax.rsqrt(var + LN_EPS) * g + b


def _mix_kernel(a_ref, gt_ref, u_ref, up_ref, un_ref, x_ref, wp_ref, ps_ref, wo_ref,
                g_ref, b_ref, o32_ref, o16_ref, pad_ref, *, alpha, seq):
    tm = u_ref.shape[0]
    d_model = a_ref.shape[1]
    n_groups = len(POOL_WINDOWS)
    gin = u_ref.shape[1] // n_groups
    i = pl.program_id(1)
    nblk = pl.num_programs(1)
    pad_ref[0:POOL_HALO, :] = jnp.where(i > 0, up_ref[...], 0.0)
    pad_ref[POOL_HALO:POOL_HALO + tm, :] = u_ref[...]
    pad_ref[POOL_HALO + tm:, :] = jnp.where(i < nblk - 1, un_ref[...], 0.0)

    t = i * tm + lax.broadcasted_iota(jnp.int32, (tm, 1), 0)
    outs = []
    for g, w in enumerate(POOL_WINDOWS):
        lo_off = -(w // 2)
        hi_off = w - 1 - w // 2
        c0 = g * gin
        tot = None
        for d in range(lo_off, hi_off + 1):
            v = pad_ref[POOL_HALO + d:POOL_HALO + d + tm, c0:c0 + gin]
            tot = v if tot is None else tot + v
        lo = jnp.maximum(t + lo_off, 0)
        hi = jnp.minimum(t + hi_off, seq - 1)
        cnt = (hi - lo + 1).astype(F32)
        pooled = tot / cnt - u_ref[:, c0:c0 + gin]
        outs.append(jnp.dot(pooled.astype(BF16), wp_ref[g], preferred_element_type=F32))
    p = jnp.concatenate(outs, axis=-1) * ps_ref[...]
    m = gt_ref[:, :d_model] * a_ref[...] + gt_ref[:, d_model:] * p
    y = jnp.dot(m.astype(BF16), wo_ref[...], preferred_element_type=F32)
    out = _layer_norm(alpha * x_ref[...] + y, g_ref[...], b_ref[...])
    o32_ref[...] = out
    o16_ref[...] = out.astype(BF16)


def _mix(a, gates, u, x, w_pool, pool_scale, w_o, ln_g, ln_b, *, alpha, tm_pref=256):
    B, S, D = x.shape
    U = u.shape[-1]
    tm = _tile(S, tm_pref)
    hb = tm // POOL_HALO
    n_halo_blocks = S // POOL_HALO
    kernel = functools.partial(_mix_kernel, alpha=alpha, seq=S)
    row = lambda b, i: (b, i, 0)
    const2 = lambda b, i: (0, 0)
    return pl.pallas_call(
        kernel,
        grid=(B, S // tm),
        in_specs=[
            pl.BlockSpec((None, tm, D), row),
            pl.BlockSpec((None, tm, 2 * D), row),
            pl.BlockSpec((None, tm, U), row),
            pl.BlockSpec((None, POOL_HALO, U),
                         lambda b, i: (b, jnp.maximum(i * hb - 1, 0), 0)),
            pl.BlockSpec((None, POOL_HALO, U),
                         lambda b, i: (b, jnp.minimum((i + 1) * hb, n_halo_blocks - 1), 0)),
            pl.BlockSpec((None, tm, D), row),
            pl.BlockSpec(w_pool.shape, lambda b, i: (0, 0, 0)),
            pl.BlockSpec((1, D), const2),
            pl.BlockSpec((D, D), const2),
            pl.BlockSpec((1, D), const2),
            pl.BlockSpec((1, D), const2),
        ],
        out_specs=[pl.BlockSpec((None, tm, D), row), pl.BlockSpec((None, tm, D), row)],
        out_shape=[jax.ShapeDtypeStruct((B, S, D), F32), jax.ShapeDtypeStruct((B, S, D), BF16)],
        scratch_shapes=[pltpu.VMEM((tm + 2 * POOL_HALO, U), F32)],
        compiler_params=_params(("parallel", "arbitrary")),
        name="mix_wo_ln",
    )(a, gates, u, u, u, x, w_pool, pool_scale, w_o, ln_g, ln_b)


def _mlp_kernel(xb_ref, x_ref, wu_ref, wd_ref, g_ref, b_ref, o32_ref, o16_ref, acc_ref, *, alpha):
    f = pl.program_id(2)

    @pl.when(f == 0)
    def _():
        acc_ref[...] = jnp.zeros(acc_ref.shape, F32)

    h = jnp.dot(xb_ref[...], wu_ref[...], preferred_element_type=F32)
    h = jnp.square(jnp.maximum(h, 0.0))
    acc_ref[...] += jnp.dot(h.astype(BF16), wd_ref[...], preferred_element_type=F32)

    @pl.when(f == pl.num_programs(2) - 1)
    def _():
        out = _layer_norm(alpha * x_ref[...] + acc_ref[...], g_ref[...], b_ref[...])
        o32_ref[...] = out
        o16_ref[...] = out.astype(BF16)


def _mlp(xb, x, w_up, w_down, ln_g, ln_b, *, alpha, tm_pref=512, tf_pref=1024):
    B, S, D = x.shape
    F = w_up.shape[1]
    tm = _tile(S, tm_pref)
    tf = _tile(F, tf_pref)
    kernel = functools.partial(_mlp_kernel, alpha=alpha)
    row = lambda b, i, f: (b, i, 0)
    const2 = lambda b, i, f: (0, 0)
    return pl.pallas_call(
        kernel,
        grid=(B, S // tm, F // tf),
        in_specs=[
            pl.BlockSpec((None, tm, D), row),
            pl.BlockSpec((None, tm, D), row),
            pl.BlockSpec((D, tf), lambda b, i, f: (0, f)),
            pl.BlockSpec((tf, D), lambda b, i, f: (f, 0)),
            pl.BlockSpec((1, D), const2),
            pl.BlockSpec((1, D), const2),
        ],
        out_specs=[pl.BlockSpec((None, tm, D), row), pl.BlockSpec((None, tm, D), row)],
        out_shape=[jax.ShapeDtypeStruct((B, S, D), F32), jax.ShapeDtypeStruct((B, S, D), BF16)],
        scratch_shapes=[pltpu.VMEM((tm, D), F32)],
        compiler_params=_params(("parallel", "parallel", "arbitrary")),
        name="mlp_ln",
    )(xb, x, w_up, w_down, ln_g, ln_b)


def _trunk(x, w):
    B, S, D = x.shape
    depth = w["w_in"].shape[0]
    alpha = (2 * depth) ** 0.25
    n_heads = D // HEAD_DIM
    kv_w = N_KV_HEADS * HEAD_DIM
    cos, sin = _rope_tables(S)
    xb = x.astype(BF16)
    for l in range(depth):
        qk, vT, u, gates = _proj(xb, w["w_in"][l], w["qk_gains"][l], cos, sin,
                                 qk_w=D + kv_w, v_w=kv_w, u_w=D // 2, tn=PROJ_TN)
        a = _flash(qk, vT, n_heads=n_heads)
        x, xb = _mix(a, gates, u, x, w["w_pool"][l], w["pool_scale"][l], w["w_o"][l],
                     w["ln1_g"][l], w["ln1_b"][l], alpha=alpha)
        x, xb = _mlp(xb, x, w["w_up"][l], w["w_down"][l], w["ln2_g"][l], w["ln2_b"][l], alpha=alpha)
    return x


def kernel(x_prompt, x_sample, w_in, q_norm, k_norm, w_pool, pool_scale, w_o,
           ln1_g, ln1_b, w_up, w_down, ln2_g, ln2_b):
    depth, d_model, _ = w_in.shape
    kv_w = N_KV_HEADS * HEAD_DIM
    qk_w = d_model + kv_w
    w_in_b = w_in.astype(BF16)
    w_in_b = jnp.concatenate([_permute_head_lanes(w_in_b[:, :, :qk_w]), w_in_b[:, :, qk_w:]], axis=-1)
    gq = _permute_head_lanes(q_norm) * SOFTMAX_SCALE_LOG2
    gk = _permute_head_lanes(k_norm)
    qk_gains = jnp.concatenate(
        [jnp.broadcast_to(gq[:, None, None, :], (depth, d_model // PROJ_TN, 1, HEAD_DIM)),
         jnp.broadcast_to(gk[:, None, None, :], (depth, kv_w // PROJ_TN, 1, HEAD_DIM))], axis=1)
    row = lambda v: v.reshape(depth, 1, v.shape[-1])
    w = {
        "w_in": w_in_b, "qk_gains": qk_gains,
        "w_pool": w_pool.astype(BF16), "pool_scale": row(pool_scale),
        "w_o": w_o.astype(BF16),
        "ln1_g": row(ln1_g), "ln1_b": row(ln1_b),
        "w_up": w_up.astype(BF16), "w_down": w_down.astype(BF16),
        "ln2_g": row(ln2_g), "ln2_b": row(ln2_b),
    }
    return (_trunk(x_prompt, w), _trunk(x_sample, w))
```

```python
import functools

import jax
import jax.numpy as jnp
from jax import lax
from jax.experimental import pallas as pl
from jax.experimental.pallas import tpu as pltpu

F32 = jnp.float32
BF16 = jnp.bfloat16

HEAD_DIM = 128
N_KV_HEADS = 4
GRID_W = 64
ROPE_THETA = 10000.0
POOL_WINDOWS = (2, 4, 8, 16)
POOL_HALO = 8
LN_EPS = 1e-5
QK_EPS = 1e-6
LOG2E = 1.4426950408889634
SOFTMAX_SCALE_LOG2 = HEAD_DIM ** -0.5 * LOG2E
NEG_BIG = -1e30
ROPE_ROW_BLOCK = 256
PROJ_TN = 512

VMEM_LIMIT_BYTES = 56 * 1024 * 1024


def _tile(n, pref):
    t = min(n, pref)
    assert n % t == 0, (n, pref)
    return t


def _params(sem):
    return pltpu.CompilerParams(dimension_semantics=sem,
                                vmem_limit_bytes=VMEM_LIMIT_BYTES)


def _permute_head_lanes(a):
    lead = a.shape[:-1]
    n = a.shape[-1] // HEAD_DIM
    a = a.reshape(lead + (n, 2, 2, HEAD_DIM // 4))
    a = jnp.swapaxes(a, -3, -2)
    return a.reshape(lead + (n * HEAD_DIM,))


def _rope_tables(seq):
    t = jnp.arange(seq)
    row = (t // GRID_W).astype(F32)
    col = (t % GRID_W).astype(F32)
    axis_dim = HEAD_DIM // 2
    inv_freq = ROPE_THETA ** (-jnp.arange(0, axis_dim, 2, dtype=F32) / axis_dim)
    ar = row[:, None] * inv_freq[None, :]
    ac = col[:, None] * inv_freq[None, :]
    cos = jnp.concatenate([jnp.cos(ar), jnp.cos(ac), jnp.cos(ar), jnp.cos(ac)], axis=-1)
    sin = jnp.concatenate([-jnp.sin(ar), -jnp.sin(ac), jnp.sin(ar), jnp.sin(ac)], axis=-1)
    return cos, sin


def _proj_kernel(x_ref, w_ref, g_ref, cos_ref, sin_ref, qk_ref, vT_ref, u_ref, gt_ref, acc_ref,
                 *, rb, j_v, j_u, j_gate):
    tm, tn = acc_ref.shape
    j = pl.program_id(2)
    heads = [slice(h * HEAD_DIM, (h + 1) * HEAD_DIM) for h in range(tn // HEAD_DIM)]
    row_blocks = [slice(r * rb, (r + 1) * rb) for r in range(tm // rb)]

    def block_matmul(rows):
        acc_ref[rows, :] = jnp.dot(x_ref[rows, :], w_ref[...], preferred_element_type=F32)

    def interleaved(epilogue):
        block_matmul(row_blocks[0])
        for r, rows in enumerate(row_blocks):
            if r + 1 < len(row_blocks):
                block_matmul(row_blocks[r + 1])
            epilogue(rows)

    def rope_epilogue(rows):
        gain = g_ref[...]
        cos = cos_ref[rows, :]
        sin = sin_ref[rows, :]
        for cols in heads:
            xh = acc_ref[rows, cols]
            ms = jnp.mean(xh * xh, axis=-1, keepdims=True)
            y = xh * lax.rsqrt(ms + QK_EPS) * gain
            rot = y * cos + pltpu.roll(y, HEAD_DIM // 2, 1) * sin
            qk_ref[rows, cols] = rot.astype(qk_ref.dtype)

    def v_epilogue(rows):
        for h, cols in enumerate(heads):
            vT_ref[h, :, rows] = acc_ref[rows, cols].T.astype(vT_ref.dtype)

    def gate_epilogue(rows):
        gt_ref[rows, :] = 1.0 / (1.0 + jnp.exp(-acc_ref[rows, :]))

    pl.when(j < j_v)(lambda: interleaved(rope_epilogue))
    pl.when(j == j_v)(lambda: interleaved(v_epilogue))
    pl.when(j >= j_gate)(lambda: interleaved(gate_epilogue))

    @pl.when((j >= j_u) & (j < j_gate))
    def _():
        u_ref[...] = jnp.dot(x_ref[...], w_ref[...], preferred_element_type=F32)


def _proj(xb, w, gains, cos, sin, *, qk_w, v_w, u_w, tm_pref=1024, tn=512):
    B, S, D = xb.shape
    N = w.shape[1]
    assert v_w == tn and qk_w % tn == 0 and u_w % tn == 0 and N % tn == 0
    tm = _tile(S, tm_pref)
    rb = _tile(tm, ROPE_ROW_BLOCK)
    j_v = qk_w // tn
    j_u = j_v + 1
    j_gate = j_u + u_w // tn
    gate_w = N - j_gate * tn
    kernel = functools.partial(_proj_kernel, rb=rb, j_v=j_v, j_u=j_u, j_gate=j_gate)
    clamp = lambda v, lo, hi: jnp.minimum(jnp.maximum(v, lo), hi)
    return pl.pallas_call(
        kernel,
        grid=(B, S // tm, N // tn),
        in_specs=[
            pl.BlockSpec((None, tm, D), lambda b, i, j: (b, i, 0)),
            pl.BlockSpec((D, tn), lambda b, i, j: (0, j)),
            pl.BlockSpec((None, 1, HEAD_DIM), lambda b, i, j: (jnp.minimum(j, j_v - 1), 0, 0)),
            pl.BlockSpec((tm, HEAD_DIM), lambda b, i, j: (i, 0)),
            pl.BlockSpec((tm, HEAD_DIM), lambda b, i, j: (i, 0)),
        ],
        out_specs=[
            pl.BlockSpec((None, tm, tn), lambda b, i, j: (b, i, jnp.minimum(j, j_v - 1))),
            pl.BlockSpec((None, tn // HEAD_DIM, HEAD_DIM, tm), lambda b, i, j: (b, 0, 0, i)),
            pl.BlockSpec((None, tm, tn), lambda b, i, j: (b, i, clamp(j - j_u, 0, j_gate - j_u - 1))),
            pl.BlockSpec((None, tm, tn), lambda b, i, j: (b, i, jnp.maximum(j - j_gate, 0))),
        ],
        out_shape=[
            jax.ShapeDtypeStruct((B, S, qk_w), BF16),
            jax.ShapeDtypeStruct((B, v_w // HEAD_DIM, HEAD_DIM, S), BF16),
            jax.ShapeDtypeStruct((B, S, u_w), F32),
            jax.ShapeDtypeStruct((B, S, gate_w), F32),
        ],
        scratch_shapes=[pltpu.VMEM((tm, tn), F32)],
        compiler_params=_params(("parallel", "parallel", "arbitrary")),
        name="proj",
    )(xb, w, gains, cos, sin)


def _flash_kernel(q_ref, qn_ref, k_ref, vT_ref, o_ref,
                  qT_ref, qTn_ref, m_ref, l_ref, acc_ref, s_ref, p_ref, mx_ref,
                  *, tk, ts, n_chunks, gh):
    G = qT_ref.shape[0]
    tq = qT_ref.shape[2]
    nt = tk // ts

    def slot_window(g):
        sb, h = divmod(g, gh)
        return slice(sb * tq, (sb + 1) * tq), slice(h * HEAD_DIM, (h + 1) * HEAD_DIM)

    def transpose_heads(src_ref, dst_ref):
        for g in range(G):
            rows, cols = slot_window(g)
            dst_ref[g] = src_ref[rows, cols].astype(F32).T.astype(BF16)

    def sub8(x):
        return x.reshape(ts // 8, 8, tq)

    def qk_tile(qt_ref, off, g, t):
        kt = k_ref[pl.ds(off + t * ts, ts), :]
        s = jnp.dot(kt, qt_ref[g], preferred_element_type=F32)
        s_ref[g, t * ts:(t + 1) * ts, :] = s
        return jnp.max(sub8(s), axis=0)

    def exp_tile(g, t, m_new):
        p = jnp.exp2(s_ref[g, t * ts:(t + 1) * ts, :] - m_new)
        p_ref[g, t * ts:(t + 1) * ts, :] = p.astype(BF16)
        return jnp.sum(sub8(p), axis=0)

    def chunk_step(off, qt_next_ref, off_next):
        for g in range(G):
            m_old = m_ref[g]
            m_new = jnp.maximum(m_old, jnp.max(mx_ref[g], axis=0, keepdims=True))
            alpha = jnp.exp2(m_old - m_new)
            m_ref[g] = m_new
            lsum = None
            mx = None
            for t in range(nt):
                ls = exp_tile(g, t, m_new)
                lsum = ls if lsum is None else lsum + ls
                tile_mx = qk_tile(qt_next_ref, off_next, g, t)
                mx = tile_mx if mx is None else jnp.maximum(mx, tile_mx)
            l_ref[g] = alpha * l_ref[g] + lsum
            acc_ref[g] = alpha * acc_ref[g] + jnp.dot(
                vT_ref[:, pl.ds(off, tk)], p_ref[g], preferred_element_type=F32)
            mx_ref[g] = mx

    m_ref[...] = jnp.full(m_ref.shape, NEG_BIG, F32)
    l_ref[...] = jnp.zeros(l_ref.shape, F32)
    acc_ref[...] = jnp.zeros(acc_ref.shape, F32)

    @pl.when(pl.program_id(2) == 0)
    def _():
        transpose_heads(q_ref, qT_ref)
        for g in range(G):
            mx = None
            for t in range(nt):
                tile_mx = qk_tile(qT_ref, 0, g, t)
                mx = tile_mx if mx is None else jnp.maximum(mx, tile_mx)
            mx_ref[g] = mx

    transpose_heads(qn_ref, qTn_ref)

    def body(c, carry):
        off = pl.multiple_of(c * tk, tk)
        chunk_step(off, qT_ref, pl.multiple_of(off + tk, tk))
        return carry

    lax.fori_loop(0, n_chunks - 1, body, 0)
    chunk_step((n_chunks - 1) * tk, qTn_ref, 0)

    for g in range(G):
        o = acc_ref[g] / jnp.sum(l_ref[g], axis=0, keepdims=True)
        rows, cols = slot_window(g)
        o_ref[rows, cols] = o.T.astype(o_ref.dtype)
    qT_ref[...] = qTn_ref[...]


def _flash(qk, vT, *, n_heads, tq_pref=256, nsub_pref=4, tk_pref=1024, ts_pref=128):
    B, KV, hd, S = vT.shape
    gh = n_heads // KV
    tq = _tile(S, tq_pref)
    nsub = _tile(S // tq, nsub_pref)
    tqb = nsub * tq
    G = nsub * gh
    tk = _tile(S, tk_pref)
    ts = _tile(tk, ts_pref)
    nq = S // tqb
    kernel = functools.partial(_flash_kernel, tk=tk, ts=ts, n_chunks=S // tk, gh=gh)
    return pl.pallas_call(
        kernel,
        grid=(B, KV, nq),
        in_specs=[
            pl.BlockSpec((None, tqb, gh * hd), lambda b, h, i: (b, i, h)),
            pl.BlockSpec((None, tqb, gh * hd), lambda b, h, i: (b, jnp.minimum(i + 1, nq - 1), h)),
            pl.BlockSpec((None, S, hd), lambda b, h, i: (b, 0, n_heads + h)),
            pl.BlockSpec((None, None, hd, S), lambda b, h, i: (b, h, 0, 0)),
        ],
        out_specs=pl.BlockSpec((None, tqb, gh * hd), lambda b, h, i: (b, i, h)),
        out_shape=jax.ShapeDtypeStruct((B, S, n_heads * hd), F32),
        scratch_shapes=[
            pltpu.VMEM((G, hd, tq), BF16),
            pltpu.VMEM((G, hd, tq), BF16),
            pltpu.VMEM((G, 1, tq), F32),
            pltpu.VMEM((G, 8, tq), F32),
            pltpu.VMEM((G, hd, tq), F32),
            pltpu.VMEM((G, tk, tq), F32),
            pltpu.VMEM((G, tk, tq), BF16),
            pltpu.VMEM((G, 8, tq), F32),
        ],
        compiler_params=_params(("parallel", "parallel", "arbitrary")),
        name="flash_attn",
    )(qk, qk, qk, vT)


def _layer_norm(z, g, b):
    mu = jnp.mean(z, axis=-1, keepdims=True)
    zc = z - mu
    var = jnp.mean(zc * zc, axis=-1, keepdims=True)
    return zc * lax.rsqrt(var + LN_EPS) * g + b


def _mix_kernel(a_ref, gt_ref, u_ref, up_ref, un_ref, x_ref, wp_ref, ps_ref, wo_ref,
                g_ref, b_ref, o32_ref, o16_ref, pad_ref, *, alpha, seq):
    tm = u_ref.shape[0]
    d_model = a_ref.shape[1]
    n_groups = len(POOL_WINDOWS)
    gin = u_ref.shape[1] // n_groups
    i = pl.program_id(1)
    nblk = pl.num_programs(1)
    pad_ref[0:POOL_HALO, :] = jnp.where(i > 0, up_ref[...], 0.0)
    pad_ref[POOL_HALO:POOL_HALO + tm, :] = u_ref[...]
    pad_ref[POOL_HALO + tm:, :] = jnp.where(i < nblk - 1, un_ref[...], 0.0)

    t = i * tm + lax.broadcasted_iota(jnp.int32, (tm, 1), 0)
    outs = []
    for g, w in enumerate(POOL_WINDOWS):
        lo_off = -(w // 2)
        hi_off = w - 1 - w // 2
        c0 = g * gin
        tot = None
        for d in range(lo_off, hi_off + 1):
            v = pad_ref[POOL_HALO + d:POOL_HALO + d + tm, c0:c0 + gin]
            tot = v if tot is None else tot + v
        lo = jnp.maximum(t + lo_off, 0)
        hi = jnp.minimum(t + hi_off, seq - 1)
        cnt = (hi - lo + 1).astype(F32)
        pooled = tot / cnt - u_ref[:, c0:c0 + gin]
        outs.append(jnp.dot(pooled.astype(BF16), wp_ref[g], preferred_element_type=F32))
    p = jnp.concatenate(outs, axis=-1) * ps_ref[...]
    m = gt_ref[:, :d_model] * a_ref[...] + gt_ref[:, d_model:] * p
    y = jnp.dot(m.astype(BF16), wo_ref[...], preferred_element_type=F32)
    out = _layer_norm(alpha * x_ref[...] + y, g_ref[...], b_ref[...])
    o32_ref[...] = out
    o16_ref[...] = out.astype(BF16)


def _mix(a, gates, u, x, w_pool, pool_scale, w_o, ln_g, ln_b, *, alpha, tm_pref=256):
    B, S, D = x.shape
    U = u.shape[-1]
    tm = _tile(S, tm_pref)
    hb = tm // POOL_HALO
    n_halo_blocks = S // POOL_HALO
    kernel = functools.partial(_mix_kernel, alpha=alpha, seq=S)
    row = lambda b, i: (b, i, 0)
    const2 = lambda b, i: (0, 0)
    return pl.pallas_call(
        kernel,
        grid=(B, S // tm),
        in_specs=[
            pl.BlockSpec((None, tm, D), row),
            pl.BlockSpec((None, tm, 2 * D), row),
            pl.BlockSpec((None, tm, U), row),
            pl.BlockSpec((None, POOL_HALO, U),
                         lambda b, i: (b, jnp.maximum(i * hb - 1, 0), 0)),
            pl.BlockSpec((None, POOL_HALO, U),
                         lambda b, i: (b, jnp.minimum((i + 1) * hb, n_halo_blocks - 1), 0)),
            pl.BlockSpec((None, tm, D), row),
            pl.BlockSpec(w_pool.shape, lambda b, i: (0, 0, 0)),
            pl.BlockSpec((1, D), const2),
            pl.BlockSpec((D, D), const2),
            pl.BlockSpec((1, D), const2),
            pl.BlockSpec((1, D), const2),
        ],
        out_specs=[pl.BlockSpec((None, tm, D), row), pl.BlockSpec((None, tm, D), row)],
        out_shape=[jax.ShapeDtypeStruct((B, S, D), F32), jax.ShapeDtypeStruct((B, S, D), BF16)],
        scratch_shapes=[pltpu.VMEM((tm + 2 * POOL_HALO, U), F32)],
        compiler_params=_params(("parallel", "arbitrary")),
        name="mix_wo_ln",
    )(a, gates, u, u, u, x, w_pool, pool_scale, w_o, ln_g, ln_b)


def _mlp_kernel(xb_ref, x_ref, wu_ref, wd_ref, g_ref, b_ref, o32_ref, o16_ref, acc_ref, *, alpha):
    f = pl.program_id(2)

    @pl.when(f == 0)
    def _():
        acc_ref[...] = jnp.zeros(acc_ref.shape, F32)

    h = jnp.dot(xb_ref[...], wu_ref[...], preferred_element_type=F32)
    h = jnp.square(jnp.maximum(h, 0.0))
    acc_ref[...] += jnp.dot(h.astype(BF16), wd_ref[...], preferred_element_type=F32)

    @pl.when(f == pl.num_programs(2) - 1)
    def _():
        out = _layer_norm(alpha * x_ref[...] + acc_ref[...], g_ref[...], b_ref[...])
        o32_ref[...] = out
        o16_ref[...] = out.astype(BF16)


def _mlp(xb, x, w_up, w_down, ln_g, ln_b, *, alpha, tm_pref=512, tf_pref=1024):
    B, S, D = x.shape
    F = w_up.shape[1]
    tm = _tile(S, tm_pref)
    tf = _tile(F, tf_pref)
    kernel = functools.partial(_mlp_kernel, alpha=alpha)
    row = lambda b, i, f: (b, i, 0)
    const2 = lambda b, i, f: (0, 0)
    return pl.pallas_call(
        kernel,
        grid=(B, S // tm, F // tf),
        in_specs=[
            pl.BlockSpec((None, tm, D), row),
            pl.BlockSpec((None, tm, D), row),
            pl.BlockSpec((D, tf), lambda b, i, f: (0, f)),
            pl.BlockSpec((tf, D), lambda b, i, f: (f, 0)),
            pl.BlockSpec((1, D), const2),
            pl.BlockSpec((1, D), const2),
        ],
        out_specs=[pl.BlockSpec((None, tm, D), row), pl.BlockSpec((None, tm, D), row)],
        out_shape=[jax.ShapeDtypeStruct((B, S, D), F32), jax.ShapeDtypeStruct((B, S, D), BF16)],
        scratch_shapes=[pltpu.VMEM((tm, D), F32)],
        compiler_params=_params(("parallel", "parallel", "arbitrary")),
        name="mlp_ln",
    )(xb, x, w_up, w_down, ln_g, ln_b)


def _trunk(x, w):
    B, S, D = x.shape
    depth = w["w_in"].shape[0]
    alpha = (2 * depth) ** 0.25
    n_heads = D // HEAD_DIM
    kv_w = N_KV_HEADS * HEAD_DIM
    cos, sin = _rope_tables(S)
    xb = x.astype(BF16)
    for l in range(depth):
        qk, vT, u, gates = _proj(xb, w["w_in"][l], w["qk_gains"][l], cos, sin,
                                 qk_w=D + kv_w, v_w=kv_w, u_w=D // 2, tn=PROJ_TN)
        a = _flash(qk, vT, n_heads=n_heads)
        x, xb = _mix(a, gates, u, x, w["w_pool"][l], w["pool_scale"][l], w["w_o"][l],
                     w["ln1_g"][l], w["ln1_b"][l], alpha=alpha)
        x, xb = _mlp(xb, x, w["w_up"][l], w["w_down"][l], w["ln2_g"][l], w["ln2_b"][l], alpha=alpha)
    return x


def kernel(x_prompt, x_sample, w_in, q_norm, k_norm, w_pool, pool_scale, w_o,
           ln1_g, ln1_b, w_up, w_down, ln2_g, ln2_b):
    depth, d_model, _ = w_in.shape
    kv_w = N_KV_HEADS * HEAD_DIM
    qk_w = d_model + kv_w
    w_in_b = w_in.astype(BF16)
    w_in_b = jnp.concatenate([_permute_head_lanes(w_in_b[:, :, :qk_w]), w_in_b[:, :, qk_w:]], axis=-1)
    gq = _permute_head_lanes(q_norm) * SOFTMAX_SCALE_LOG2
    gk = _permute_head_lanes(k_norm)
    qk_gains = jnp.concatenate(
        [jnp.broadcast_to(gq[:, None, None, :], (depth, d_model // PROJ_TN, 1, HEAD_DIM)),
         jnp.broadcast_to(gk[:, None, None, :], (depth, kv_w // PROJ_TN, 1, HEAD_DIM))], axis=1)
    row = lambda v: v.reshape(depth, 1, v.shape[-1])
    w = {
        "w_in": w_in_b, "qk_gains": qk_gains,
        "w_pool": w_pool.astype(BF16), "pool_scale": row(pool_scale),
        "w_o": w_o.astype(BF16),
        "ln1_g": row(ln1_g), "ln1_b": row(ln1_b),
        "w_up": w_up.astype(BF16), "w_down": w_down.astype(BF16),
        "ln2_g": row(ln2_g), "ln2_b": row(ln2_b),
    }
    return (_trunk(x_prompt, w), _trunk(x_sample, w))
```

```python
import functools

import jax
import jax.numpy as jnp
from jax import lax
from jax.experimental import pallas as pl
from jax.experimental.pallas import tpu as pltpu

F32 = jnp.float32
BF16 = jnp.bfloat16

HEAD_DIM = 128
N_KV_HEADS = 4
GRID_W = 64
ROPE_THETA = 10000.0
POOL_WINDOWS = (2, 4, 8, 16)
POOL_HALO = 8
LN_EPS = 1e-5
QK_EPS = 1e-6
LOG2E = 1.4426950408889634
SOFTMAX_SCALE_LOG2 = HEAD_DIM ** -0.5 * LOG2E
NEG_BIG = -1e30
ROPE_ROW_BLOCK = 256
PROJ_TN = 512

VMEM_LIMIT_BYTES = 56 * 1024 * 1024


def _tile(n, pref):
    t = min(n, pref)
    assert n % t == 0, (n, pref)
    return t


def _params(sem):
    return pltpu.CompilerParams(dimension_semantics=sem,
                                vmem_limit_bytes=VMEM_LIMIT_BYTES)


def _permute_head_lanes(a):
    lead = a.shape[:-1]
    n = a.shape[-1] // HEAD_DIM
    a = a.reshape(lead + (n, 2, 2, HEAD_DIM // 4))
    a = jnp.swapaxes(a, -3, -2)
    return a.reshape(lead + (n * HEAD_DIM,))


def _rope_tables(seq):
    t = jnp.arange(seq)
    row = (t // GRID_W).astype(F32)
    col = (t % GRID_W).astype(F32)
    axis_dim = HEAD_DIM // 2
    inv_freq = ROPE_THETA ** (-jnp.arange(0, axis_dim, 2, dtype=F32) / axis_dim)
    ar = row[:, None] * inv_freq[None, :]
    ac = col[:, None] * inv_freq[None, :]
    cos = jnp.concatenate([jnp.cos(ar), jnp.cos(ac), jnp.cos(ar), jnp.cos(ac)], axis=-1)
    sin = jnp.concatenate([-jnp.sin(ar), -jnp.sin(ac), jnp.sin(ar), jnp.sin(ac)], axis=-1)
    return cos, sin


def _proj_kernel(x_ref, w_ref, g_ref, cos_ref, sin_ref, qk_ref, vT_ref, u_ref, gt_ref, acc_ref,
                 *maybe_xb_ref, rb, j_v, j_u, j_gate):
    tm, tn = acc_ref.shape
    j = pl.program_id(2)
    heads = [slice(h * HEAD_DIM, (h + 1) * HEAD_DIM) for h in range(tn // HEAD_DIM)]
    row_blocks = [slice(r * rb, (r + 1) * rb) for r in range(tm // rb)]

    lhs_ref = x_ref
    if maybe_xb_ref:
        (lhs_ref,) = maybe_xb_ref

        @pl.when(j == 0)
        def _():
            lhs_ref[...] = x_ref[...].astype(lhs_ref.dtype)

    def block_matmul(rows):
        acc_ref[rows, :] = jnp.dot(lhs_ref[rows, :], w_ref[...], preferred_element_type=F32)

    def interleaved(epilogue):
        block_matmul(row_blocks[0])
        for r, rows in enumerate(row_blocks):
            if r + 1 < len(row_blocks):
                block_matmul(row_blocks[r + 1])
            epilogue(rows)

    def rope_epilogue(rows):
        gain = g_ref[...]
        cos = cos_ref[rows, :]
        sin = sin_ref[rows, :]
        for cols in heads:
            xh = acc_ref[rows, cols]
            ms = jnp.mean(xh * xh, axis=-1, keepdims=True)
            y = xh * lax.rsqrt(ms + QK_EPS) * gain
            rot = y * cos + pltpu.roll(y, HEAD_DIM // 2, 1) * sin
            qk_ref[rows, cols] = rot.astype(qk_ref.dtype)

    def v_epilogue(rows):
        for h, cols in enumerate(heads):
            vT_ref[h, :, rows] = acc_ref[rows, cols].T.astype(vT_ref.dtype)

    def gate_epilogue(rows):
        gt_ref[rows, :] = 1.0 / (1.0 + jnp.exp(-acc_ref[rows, :]))

    pl.when(j < j_v)(lambda: interleaved(rope_epilogue))
    pl.when(j == j_v)(lambda: interleaved(v_epilogue))
    pl.when(j >= j_gate)(lambda: interleaved(gate_epilogue))

    @pl.when((j >= j_u) & (j < j_gate))
    def _():
        u_ref[...] = jnp.dot(lhs_ref[...], w_ref[...], preferred_element_type=F32)


def _proj(x, w, gains, cos, sin, *, layer, qk_w, v_w, u_w, tm_pref=1024, tn=512):
    B, S, D = x.shape
    N = w.shape[2]
    assert v_w == tn and qk_w % tn == 0 and u_w % tn == 0 and N % tn == 0
    tm = _tile(S, tm_pref)
    rb = _tile(tm, ROPE_ROW_BLOCK)
    j_v = qk_w // tn
    j_u = j_v + 1
    j_gate = j_u + u_w // tn
    gate_w = N - j_gate * tn
    kernel = functools.partial(_proj_kernel, rb=rb, j_v=j_v, j_u=j_u, j_gate=j_gate)
    clamp = lambda v, lo, hi: jnp.minimum(jnp.maximum(v, lo), hi)
    return pl.pallas_call(
        kernel,
        grid=(B, S // tm, N // tn),
        in_specs=[
            pl.BlockSpec((None, tm, D), lambda b, i, j: (b, i, 0)),
            pl.BlockSpec((None, D, tn), lambda b, i, j: (layer, 0, j)),
            pl.BlockSpec((None, None, 1, HEAD_DIM),
                         lambda b, i, j: (layer, jnp.minimum(j, j_v - 1), 0, 0)),
            pl.BlockSpec((tm, HEAD_DIM), lambda b, i, j: (i, 0)),
            pl.BlockSpec((tm, HEAD_DIM), lambda b, i, j: (i, 0)),
        ],
        out_specs=[
            pl.BlockSpec((None, tm, tn), lambda b, i, j: (b, i, jnp.minimum(j, j_v - 1))),
            pl.BlockSpec((None, tn // HEAD_DIM, HEAD_DIM, tm), lambda b, i, j: (b, 0, 0, i)),
            pl.BlockSpec((None, tm, tn), lambda b, i, j: (b, i, clamp(j - j_u, 0, j_gate - j_u - 1))),
            pl.BlockSpec((None, tm, tn), lambda b, i, j: (b, i, jnp.maximum(j - j_gate, 0))),
        ],
        out_shape=[
            jax.ShapeDtypeStruct((B, S, qk_w), BF16),
            jax.ShapeDtypeStruct((B, v_w // HEAD_DIM, HEAD_DIM, S), BF16),
            jax.ShapeDtypeStruct((B, S, u_w), F32),
            jax.ShapeDtypeStruct((B, S, gate_w), F32),
        ],
        scratch_shapes=[pltpu.VMEM((tm, tn), F32)]
        + ([] if x.dtype == BF16 else [pltpu.VMEM((tm, D), BF16)]),
        compiler_params=_params(("parallel", "parallel", "arbitrary")),
        name="proj",
    )(x, w, gains, cos, sin)


def _flash_kernel(q_ref, qn_ref, k_ref, vT_ref, o_ref,
                  qT_ref, qTn_ref, m_ref, l_ref, acc_ref, s_ref, p_ref, mx_ref,
                  *, tk, ts, n_chunks, gh):
    G = qT_ref.shape[0]
    tq = qT_ref.shape[2]
    nt = tk // ts

    def slot_window(g):
        sb, h = divmod(g, gh)
        return slice(sb * tq, (sb + 1) * tq), slice(h * HEAD_DIM, (h + 1) * HEAD_DIM)

    def transpose_heads(src_ref, dst_ref):
        for g in range(G):
            rows, cols = slot_window(g)
            dst_ref[g] = src_ref[rows, cols].astype(F32).T.astype(BF16)

    def sub8(x):
        return x.reshape(ts // 8, 8, tq)

    def qk_tile(qt_ref, off, g, t):
        kt = k_ref[pl.ds(off + t * ts, ts), :]
        s = jnp.dot(kt, qt_ref[g], preferred_element_type=F32)
        s_ref[g, t * ts:(t + 1) * ts, :] = s
        return jnp.max(sub8(s), axis=0)

    def exp_tile(g, t, m_new):
        p = jnp.exp2(s_ref[g, t * ts:(t + 1) * ts, :] - m_new)
        p_ref[g, t * ts:(t + 1) * ts, :] = p.astype(BF16)
        return jnp.sum(sub8(p), axis=0)

    def chunk_step(off, qt_next_ref, off_next):
        for g in range(G):
            m_old = m_ref[g]
            m_new = jnp.maximum(m_old, jnp.max(mx_ref[g], axis=0, keepdims=True))
            alpha = jnp.exp2(m_old - m_new)
            m_ref[g] = m_new
            lsum = None
            mx = None
            for t in range(nt):
                ls = exp_tile(g, t, m_new)
                lsum = ls if lsum is None else lsum + ls
                tile_mx = qk_tile(qt_next_ref, off_next, g, t)
                mx = tile_mx if mx is None else jnp.maximum(mx, tile_mx)
            l_ref[g] = alpha * l_ref[g] + lsum
            acc_ref[g] = alpha * acc_ref[g] + jnp.dot(
                vT_ref[:, pl.ds(off, tk)], p_ref[g], preferred_element_type=F32)
            mx_ref[g] = mx

    m_ref[...] = jnp.full(m_ref.shape, NEG_BIG, F32)
    l_ref[...] = jnp.zeros(l_ref.shape, F32)
    acc_ref[...] = jnp.zeros(acc_ref.shape, F32)

    @pl.when(pl.program_id(2) == 0)
    def _():
        transpose_heads(q_ref, qT_ref)
        for g in range(G):
            mx = None
            for t in range(nt):
                tile_mx = qk_tile(qT_ref, 0, g, t)
                mx = tile_mx if mx is None else jnp.maximum(mx, tile_mx)
            mx_ref[g] = mx

    transpose_heads(qn_ref, qTn_ref)

    def body(c, carry):
        off = pl.multiple_of(c * tk, tk)
        chunk_step(off, qT_ref, pl.multiple_of(off + tk, tk))
        return carry

    lax.fori_loop(0, n_chunks - 1, body, 0)
    chunk_step((n_chunks - 1) * tk, qTn_ref, 0)

    for g in range(G):
        o = acc_ref[g] / jnp.sum(l_ref[g], axis=0, keepdims=True)
        rows, cols = slot_window(g)
        o_ref[rows, cols] = o.T.astype(o_ref.dtype)
    qT_ref[...] = qTn_ref[...]


def _flash(qk, vT, *, n_heads, tq_pref=256, nsub_pref=4, tk_pref=1024, ts_pref=128):
    B, KV, hd, S = vT.shape
    gh = n_heads // KV
    tq = _tile(S, tq_pref)
    nsub = _tile(S // tq, nsub_pref)
    tqb = nsub * tq
    G = nsub * gh
    tk = _tile(S, tk_pref)
    ts = _tile(tk, ts_pref)
    nq = S // tqb
    kernel = functools.partial(_flash_kernel, tk=tk, ts=ts, n_chunks=S // tk, gh=gh)
    return pl.pallas_call(
        kernel,
        grid=(B, KV, nq),
        in_specs=[
            pl.BlockSpec((None, tqb, gh * hd), lambda b, h, i: (b, i, h)),
            pl.BlockSpec((None, tqb, gh * hd), lambda b, h, i: (b, jnp.minimum(i + 1, nq - 1), h)),
            pl.BlockSpec((None, S, hd), lambda b, h, i: (b, 0, n_heads + h)),
            pl.BlockSpec((None, None, hd, S), lambda b, h, i: (b, h, 0, 0)),
        ],
        out_specs=pl.BlockSpec((None, tqb, gh * hd), lambda b, h, i: (b, i, h)),
        out_shape=jax.ShapeDtypeStruct((B, S, n_heads * hd), F32),
        scratch_shapes=[
            pltpu.VMEM((G, hd, tq), BF16),
            pltpu.VMEM((G, hd, tq), BF16),
            pltpu.VMEM((G, 1, tq), F32),
            pltpu.VMEM((G, 8, tq), F32),
            pltpu.VMEM((G, hd, tq), F32),
            pltpu.VMEM((G, tk, tq), F32),
            pltpu.VMEM((G, tk, tq), BF16),
            pltpu.VMEM((G, 8, tq), F32),
        ],
        compiler_params=_params(("parallel", "parallel", "arbitrary")),
        name="flash_attn",
    )(qk, qk, qk, vT)


def _layer_norm(z, g, b):
    mu = jnp.mean(z, axis=-1, keepdims=True)
    zc = z - mu
    var = jnp.mean(zc * zc, axis=-1, keepdims=True)
    return zc * lax.rsqrt(var + LN_EPS) * g + b


def _mix_kernel(a_ref, gt_ref, u_ref, up_ref, un_ref, x_ref, wp_ref, ps_ref, wo_ref,
                g_ref, b_ref, o32_ref, o16_ref, pad_ref, *, alpha, seq):
    tm = u_ref.shape[0]
    d_model = a_ref.shape[1]
    n_groups = len(POOL_WINDOWS)
    gin = u_ref.shape[1] // n_groups
    i = pl.program_id(1)
    nblk = pl.num_programs(1)
    pad_ref[0:POOL_HALO, :] = jnp.where(i > 0, up_ref[...], 0.0)
    pad_ref[POOL_HALO:POOL_HALO + tm, :] = u_ref[...]
    pad_ref[POOL_HALO + tm:, :] = jnp.where(i < nblk - 1, un_ref[...], 0.0)

    t = i * tm + lax.broadcasted_iota(jnp.int32, (tm, 1), 0)
    outs = []
    for g, w in enumerate(POOL_WINDOWS):
        lo_off = -(w // 2)
        hi_off = w - 1 - w // 2
        c0 = g * gin
        tot = None
        for d in range(lo_off, hi_off + 1):
            v = pad_ref[POOL_HALO + d:POOL_HALO + d + tm, c0:c0 + gin]
            tot = v if tot is None else tot + v
        lo = jnp.maximum(t + lo_off, 0)
        hi = jnp.minimum(t + hi_off, seq - 1)
        cnt = (hi - lo + 1).astype(F32)
        pooled = tot / cnt - u_ref[:, c0:c0 + gin]
        outs.append(jnp.dot(pooled.astype(BF16), wp_ref[g], preferred_element_type=F32))
    p = jnp.concatenate(outs, axis=-1) * ps_ref[...]
    m = gt_ref[:, :d_model] * a_ref[...] + gt_ref[:, d_model:] * p
    y = jnp.dot(m.astype(BF16), wo_ref[...], preferred_element_type=F32)
    out = _layer_norm(alpha * x_ref[...] + y, g_ref[...], b_ref[...])
    o32_ref[...] = out
    o16_ref[...] = out.astype(BF16)


def _mix(a, gates, u, x, w_pool, pool_scale, w_o, ln_g, ln_b, *, layer, alpha, tm_pref=256):
    B, S, D = x.shape
    U = u.shape[-1]
    tm = _tile(S, tm_pref)
    hb = tm // POOL_HALO
    n_halo_blocks = S // POOL_HALO
    kernel = functools.partial(_mix_kernel, alpha=alpha, seq=S)
    row = lambda b, i: (b, i, 0)
    vec = pl.BlockSpec((None, 1, D), lambda b, i: (layer, 0, 0))
    return pl.pallas_call(
        kernel,
        grid=(B, S // tm),
        in_specs=[
            pl.BlockSpec((None, tm, D), row),
            pl.BlockSpec((None, tm, 2 * D), row),
            pl.BlockSpec((None, tm, U), row),
            pl.BlockSpec((None, POOL_HALO, U),
                         lambda b, i: (b, jnp.maximum(i * hb - 1, 0), 0)),
            pl.BlockSpec((None, POOL_HALO, U),
                         lambda b, i: (b, jnp.minimum((i + 1) * hb, n_halo_blocks - 1), 0)),
            pl.BlockSpec((None, tm, D), row),
            pl.BlockSpec((None,) + w_pool.shape[1:], lambda b, i: (layer, 0, 0, 0)),
            vec,
            pl.BlockSpec((None, D, D), lambda b, i: (layer, 0, 0)),
            vec,
            vec,
        ],
        out_specs=[pl.BlockSpec((None, tm, D), row), pl.BlockSpec((None, tm, D), row)],
        out_shape=[jax.ShapeDtypeStruct((B, S, D), F32), jax.ShapeDtypeStruct((B, S, D), BF16)],
        scratch_shapes=[pltpu.VMEM((tm + 2 * POOL_HALO, U), F32)],
        compiler_params=_params(("parallel", "arbitrary")),
        name="mix_wo_ln",
    )(a, gates, u, u, u, x, w_pool, pool_scale, w_o, ln_g, ln_b)


def _mlp_kernel(xb_ref, x_ref, wu_ref, wd_ref, g_ref, b_ref, o32_ref, o16_ref, acc_ref, *, alpha):
    f = pl.program_id(2)

    @pl.when(f == 0)
    def _():
        acc_ref[...] = jnp.zeros(acc_ref.shape, F32)

    h = jnp.dot(xb_ref[...], wu_ref[...], preferred_element_type=F32)
    h = jnp.square(jnp.maximum(h, 0.0))
    acc_ref[...] += jnp.dot(h.astype(BF16), wd_ref[...], preferred_element_type=F32)

    @pl.when(f == pl.num_programs(2) - 1)
    def _():
        out = _layer_norm(alpha * x_ref[...] + acc_ref[...], g_ref[...], b_ref[...])
        o32_ref[...] = out
        o16_ref[...] = out.astype(BF16)


def _mlp(xb, x, w_up, w_down, ln_g, ln_b, *, layer, alpha, tm_pref=512, tf_pref=1024):
    B, S, D = x.shape
    F = w_up.shape[2]
    tm = _tile(S, tm_pref)
    tf = _tile(F, tf_pref)
    kernel = functools.partial(_mlp_kernel, alpha=alpha)
    row = lambda b, i, f: (b, i, 0)
    vec = pl.BlockSpec((None, 1, D), lambda b, i, f: (layer, 0, 0))
    return pl.pallas_call(
        kernel,
        grid=(B, S // tm, F // tf),
        in_specs=[
            pl.BlockSpec((None, tm, D), row),
            pl.BlockSpec((None, tm, D), row),
            pl.BlockSpec((None, D, tf), lambda b, i, f: (layer, 0, f)),
            pl.BlockSpec((None, tf, D), lambda b, i, f: (layer, f, 0)),
            vec,
            vec,
        ],
        out_specs=[pl.BlockSpec((None, tm, D), row), pl.BlockSpec((None, tm, D), row)],
        out_shape=[jax.ShapeDtypeStruct((B, S, D), F32), jax.ShapeDtypeStruct((B, S, D), BF16)],
        scratch_shapes=[pltpu.VMEM((tm, D), F32)],
        compiler_params=_params(("parallel", "parallel", "arbitrary")),
        name="mlp_ln",
    )(xb, x, w_up, w_down, ln_g, ln_b)


def _trunk(x, w):
    B, S, D = x.shape
    depth = w["w_in"].shape[0]
    alpha = (2 * depth) ** 0.25
    n_heads = D // HEAD_DIM
    kv_w = N_KV_HEADS * HEAD_DIM
    cos, sin = _rope_tables(S)
    xb = x
    for l in range(depth):
        qk, vT, u, gates = _proj(xb, w["w_in"], w["qk_gains"], cos, sin, layer=l,
                                 qk_w=D + kv_w, v_w=kv_w, u_w=D // 2, tn=PROJ_TN)
        a = _flash(qk, vT, n_heads=n_heads)
        x, xb = _mix(a, gates, u, x, w["w_pool"], w["pool_scale"], w["w_o"],
                     w["ln1_g"], w["ln1_b"], layer=l, alpha=alpha)
        x, xb = _mlp(xb, x, w["w_up"], w["w_down"], w["ln2_g"], w["ln2_b"], layer=l, alpha=alpha)
    return x


def kernel(x_prompt, x_sample, w_in, q_norm, k_norm, w_pool, pool_scale, w_o,
           ln1_g, ln1_b, w_up, w_down, ln2_g, ln2_b):
    depth, d_model, _ = w_in.shape
    kv_w = N_KV_HEADS * HEAD_DIM
    qk_w = d_model + kv_w
    w_in_b = w_in.astype(BF16)
    w_in_b = jnp.concatenate([_permute_head_lanes(w_in_b[:, :, :qk_w]), w_in_b[:, :, qk_w:]], axis=-1)
    gq = _permute_head_lanes(q_norm) * SOFTMAX_SCALE_LOG2
    gk = _permute_head_lanes(k_norm)
    qk_gains = jnp.concatenate(
        [jnp.broadcast_to(gq[:, None, None, :], (depth, d_model // PROJ_TN, 1, HEAD_DIM)),
         jnp.broadcast_to(gk[:, None, None, :], (depth, kv_w // PROJ_TN, 1, HEAD_DIM))], axis=1)
    row = lambda v: v.reshape(depth, 1, v.shape[-1])
    w = {
        "w_in": w_in_b, "qk_gains": qk_gains,
        "w_pool": w_pool.astype(BF16), "pool_scale": row(pool_scale),
        "w_o": w_o.astype(BF16),
        "ln1_g": row(ln1_g), "ln1_b": row(ln1_b),
        "w_up": w_up.astype(BF16), "w_down": w_down.astype(BF16),
        "ln2_g": row(ln2_g), "ln2_b": row(ln2_b),
    }
    return (_trunk(x_prompt, w), _trunk(x_sample, w))
```

```python
import functools

import jax
import jax.numpy as jnp
from jax import lax
from jax.experimental import pallas as pl
from jax.experimental.pallas import tpu as pltpu

F32 = jnp.float32
BF16 = jnp.bfloat16

HEAD_DIM = 128
N_KV_HEADS = 4
GRID_W = 64
ROPE_THETA = 10000.0
POOL_WINDOWS = (2, 4, 8, 16)
POOL_HALO = 8
assert all(w >= 2 and w & (w - 1) == 0 and w // 2 <= POOL_HALO for w in POOL_WINDOWS)
LN_EPS = 1e-5
QK_EPS = 1e-6
LOG2E = 1.4426950408889634
SOFTMAX_SCALE_LOG2 = HEAD_DIM ** -0.5 * LOG2E
NEG_BIG = -1e30
ROPE_ROW_BLOCK = 256
PROJ_TN = 512

VMEM_LIMIT_BYTES = 56 * 1024 * 1024


def _tile(n, pref):
    t = min(n, pref)
    assert n % t == 0, (n, pref)
    return t


def _params(sem):
    return pltpu.CompilerParams(dimension_semantics=sem,
                                vmem_limit_bytes=VMEM_LIMIT_BYTES)


def _permute_head_lanes(a):
    lead = a.shape[:-1]
    n = a.shape[-1] // HEAD_DIM
    a = a.reshape(lead + (n, 2, 2, HEAD_DIM // 4))
    a = jnp.swapaxes(a, -3, -2)
    return a.reshape(lead + (n * HEAD_DIM,))


def _rope_tables(seq):
    t = jnp.arange(seq)
    row = (t // GRID_W).astype(F32)
    col = (t % GRID_W).astype(F32)
    axis_dim = HEAD_DIM // 2
    inv_freq = ROPE_THETA ** (-jnp.arange(0, axis_dim, 2, dtype=F32) / axis_dim)
    ar = row[:, None] * inv_freq[None, :]
    ac = col[:, None] * inv_freq[None, :]
    cos = jnp.concatenate([jnp.cos(ar), jnp.cos(ac), jnp.cos(ar), jnp.cos(ac)], axis=-1)
    sin = jnp.concatenate([-jnp.sin(ar), -jnp.sin(ac), jnp.sin(ar), jnp.sin(ac)], axis=-1)
    return cos, sin


def _proj_kernel(x_ref, w_ref, g_ref, cos_ref, sin_ref, qk_ref, vT_ref, u_ref, gt_ref, acc_ref,
                 *maybe_xb_ref, rb, j_v, j_u, j_gate):
    tm, tn = acc_ref.shape
    j = pl.program_id(2)
    heads = [slice(h * HEAD_DIM, (h + 1) * HEAD_DIM) for h in range(tn // HEAD_DIM)]
    row_blocks = [slice(r * rb, (r + 1) * rb) for r in range(tm // rb)]

    lhs_ref = x_ref
    if maybe_xb_ref:
        (lhs_ref,) = maybe_xb_ref

        @pl.when(j == 0)
        def _():
            lhs_ref[...] = x_ref[...].astype(lhs_ref.dtype)

    def block_matmul(rows):
        acc_ref[rows, :] = jnp.dot(lhs_ref[rows, :], w_ref[...], preferred_element_type=F32)

    def interleaved(epilogue):
        block_matmul(row_blocks[0])
        for r, rows in enumerate(row_blocks):
            if r + 1 < len(row_blocks):
                block_matmul(row_blocks[r + 1])
            epilogue(rows)

    def rope_epilogue(rows):
        gain = g_ref[...]
        cos = cos_ref[rows, :]
        sin = sin_ref[rows, :]
        for cols in heads:
            xh = acc_ref[rows, cols]
            ms = jnp.mean(xh * xh, axis=-1, keepdims=True)
            y = xh * lax.rsqrt(ms + QK_EPS) * gain
            rot = y * cos + pltpu.roll(y, HEAD_DIM // 2, 1) * sin
            qk_ref[rows, cols] = rot.astype(qk_ref.dtype)

    def v_epilogue(rows):
        for h, cols in enumerate(heads):
            vT_ref[h, :, rows] = acc_ref[rows, cols].T.astype(vT_ref.dtype)

    def gate_epilogue(rows):
        gt_ref[rows, :] = 1.0 / (1.0 + jnp.exp(-acc_ref[rows, :]))

    pl.when(j < j_v)(lambda: interleaved(rope_epilogue))
    pl.when(j == j_v)(lambda: interleaved(v_epilogue))
    pl.when(j >= j_gate)(lambda: interleaved(gate_epilogue))

    @pl.when((j >= j_u) & (j < j_gate))
    def _():
        u_ref[...] = jnp.dot(lhs_ref[...], w_ref[...], preferred_element_type=F32)


def _proj(x, w, gains, cos, sin, *, layer, qk_w, v_w, u_w, tm_pref=1024, tn=512):
    B, S, D = x.shape
    N = w.shape[2]
    assert v_w == tn and qk_w % tn == 0 and u_w % tn == 0 and N % tn == 0
    tm = _tile(S, tm_pref)
    rb = _tile(tm, ROPE_ROW_BLOCK)
    j_v = qk_w // tn
    j_u = j_v + 1
    j_gate = j_u + u_w // tn
    gate_w = N - j_gate * tn
    kernel = functools.partial(_proj_kernel, rb=rb, j_v=j_v, j_u=j_u, j_gate=j_gate)
    clamp = lambda v, lo, hi: jnp.minimum(jnp.maximum(v, lo), hi)
    return pl.pallas_call(
        kernel,
        grid=(B, S // tm, N // tn),
        in_specs=[
            pl.BlockSpec((None, tm, D), lambda b, i, j: (b, i, 0)),
            pl.BlockSpec((None, D, tn), lambda b, i, j: (layer, 0, j)),
            pl.BlockSpec((None, None, 1, HEAD_DIM),
                         lambda b, i, j: (layer, jnp.minimum(j, j_v - 1), 0, 0)),
            pl.BlockSpec((tm, HEAD_DIM), lambda b, i, j: (i, 0)),
            pl.BlockSpec((tm, HEAD_DIM), lambda b, i, j: (i, 0)),
        ],
        out_specs=[
            pl.BlockSpec((None, tm, tn), lambda b, i, j: (b, i, jnp.minimum(j, j_v - 1))),
            pl.BlockSpec((None, tn // HEAD_DIM, HEAD_DIM, tm), lambda b, i, j: (b, 0, 0, i)),
            pl.BlockSpec((None, tm, tn), lambda b, i, j: (b, i, clamp(j - j_u, 0, j_gate - j_u - 1))),
            pl.BlockSpec((None, tm, tn), lambda b, i, j: (b, i, jnp.maximum(j - j_gate, 0))),
        ],
        out_shape=[
            jax.ShapeDtypeStruct((B, S, qk_w), BF16),
            jax.ShapeDtypeStruct((B, v_w // HEAD_DIM, HEAD_DIM, S), BF16),
            jax.ShapeDtypeStruct((B, S, u_w), F32),
            jax.ShapeDtypeStruct((B, S, gate_w), F32),
        ],
        scratch_shapes=[pltpu.VMEM((tm, tn), F32)]
        + ([] if x.dtype == BF16 else [pltpu.VMEM((tm, D), BF16)]),
        compiler_params=_params(("parallel", "parallel", "arbitrary")),
        name="proj",
    )(x, w, gains, cos, sin)


def _flash_kernel(q_ref, qn_ref, k_ref, vT_ref, o_ref,
                  qT_ref, qTn_ref, m_ref, l_ref, acc_ref, s_ref, p_ref, mx_ref,
                  *, tk, ts, n_chunks, gh):
    G = qT_ref.shape[0]
    tq = qT_ref.shape[2]
    nt = tk // ts

    def slot_window(g):
        sb, h = divmod(g, gh)
        return slice(sb * tq, (sb + 1) * tq), slice(h * HEAD_DIM, (h + 1) * HEAD_DIM)

    def transpose_heads(src_ref, dst_ref):
        for g in range(G):
            rows, cols = slot_window(g)
            dst_ref[g] = src_ref[rows, cols].astype(F32).T.astype(BF16)

    def sub8(x):
        return x.reshape(ts // 8, 8, tq)

    def qk_tile(qt_ref, off, g, t):
        kt = k_ref[pl.ds(off + t * ts, ts), :]
        s = jnp.dot(kt, qt_ref[g], preferred_element_type=F32)
        s_ref[g, t * ts:(t + 1) * ts, :] = s
        return jnp.max(sub8(s), axis=0)

    def exp_tile(g, t, m_new):
        p = jnp.exp2(s_ref[g, t * ts:(t + 1) * ts, :] - m_new)
        p_ref[g, t * ts:(t + 1) * ts, :] = p.astype(BF16)
        return jnp.sum(sub8(p), axis=0)

    def chunk_step(off, qt_next_ref, off_next):
        for g in range(G):
            m_old = m_ref[g]
            m_new = jnp.maximum(m_old, jnp.max(mx_ref[g], axis=0, keepdims=True))
            alpha = jnp.exp2(m_old - m_new)
            m_ref[g] = m_new
            lsum = None
            mx = None
            for t in range(nt):
                ls = exp_tile(g, t, m_new)
                lsum = ls if lsum is None else lsum + ls
                tile_mx = qk_tile(qt_next_ref, off_next, g, t)
                mx = tile_mx if mx is None else jnp.maximum(mx, tile_mx)
            l_ref[g] = alpha * l_ref[g] + lsum
            acc_ref[g] = alpha * acc_ref[g] + jnp.dot(
                vT_ref[:, pl.ds(off, tk)], p_ref[g], preferred_element_type=F32)
            mx_ref[g] = mx

    m_ref[...] = jnp.full(m_ref.shape, NEG_BIG, F32)
    l_ref[...] = jnp.zeros(l_ref.shape, F32)
    acc_ref[...] = jnp.zeros(acc_ref.shape, F32)

    @pl.when(pl.program_id(2) == 0)
    def _():
        transpose_heads(q_ref, qT_ref)
        for g in range(G):
            mx = None
            for t in range(nt):
                tile_mx = qk_tile(qT_ref, 0, g, t)
                mx = tile_mx if mx is None else jnp.maximum(mx, tile_mx)
            mx_ref[g] = mx

    transpose_heads(qn_ref, qTn_ref)

    def body(c, carry):
        off = pl.multiple_of(c * tk, tk)
        chunk_step(off, qT_ref, pl.multiple_of(off + tk, tk))
        return carry

    lax.fori_loop(0, n_chunks - 1, body, 0)
    chunk_step((n_chunks - 1) * tk, qTn_ref, 0)

    for g in range(G):
        o = acc_ref[g] / jnp.sum(l_ref[g], axis=0, keepdims=True)
        rows, cols = slot_window(g)
        o_ref[rows, cols] = o.T.astype(o_ref.dtype)
    qT_ref[...] = qTn_ref[...]


def _flash(qk, vT, *, n_heads, tq_pref=256, nsub_pref=4, tk_pref=1024, ts_pref=128):
    B, KV, hd, S = vT.shape
    gh = n_heads // KV
    tq = _tile(S, tq_pref)
    nsub = _tile(S // tq, nsub_pref)
    tqb = nsub * tq
    G = nsub * gh
    tk = _tile(S, tk_pref)
    ts = _tile(tk, ts_pref)
    nq = S // tqb
    kernel = functools.partial(_flash_kernel, tk=tk, ts=ts, n_chunks=S // tk, gh=gh)
    return pl.pallas_call(
        kernel,
        grid=(B, KV, nq),
        in_specs=[
            pl.BlockSpec((None, tqb, gh * hd), lambda b, h, i: (b, i, h)),
            pl.BlockSpec((None, tqb, gh * hd), lambda b, h, i: (b, jnp.minimum(i + 1, nq - 1), h)),
            pl.BlockSpec((None, S, hd), lambda b, h, i: (b, 0, n_heads + h)),
            pl.BlockSpec((None, None, hd, S), lambda b, h, i: (b, h, 0, 0)),
        ],
        out_specs=pl.BlockSpec((None, tqb, gh * hd), lambda b, h, i: (b, i, h)),
        out_shape=jax.ShapeDtypeStruct((B, S, n_heads * hd), F32),
        scratch_shapes=[
            pltpu.VMEM((G, hd, tq), BF16),
            pltpu.VMEM((G, hd, tq), BF16),
            pltpu.VMEM((G, 1, tq), F32),
            pltpu.VMEM((G, 8, tq), F32),
            pltpu.VMEM((G, hd, tq), F32),
            pltpu.VMEM((G, tk, tq), F32),
            pltpu.VMEM((G, tk, tq), BF16),
            pltpu.VMEM((G, 8, tq), F32),
        ],
        compiler_params=_params(("parallel", "parallel", "arbitrary")),
        name="flash_attn",
    )(qk, qk, qk, vT)


def _layer_norm(z, g, b):
    mu = jnp.mean(z, axis=-1, keepdims=True)
    zc = z - mu
    var = jnp.mean(zc * zc, axis=-1, keepdims=True)
    return zc * lax.rsqrt(var + LN_EPS) * g + b


def _mix_kernel(a_ref, gt_ref, u_ref, up_ref, un_ref, x_ref, wp_ref, ps_ref, wo_ref,
                g_ref, b_ref, o32_ref, o16_ref, pad_ref, *, alpha, seq):
    tm = u_ref.shape[0]
    d_model = a_ref.shape[1]
    n_groups = len(POOL_WINDOWS)
    gin = u_ref.shape[1] // n_groups
    i = pl.program_id(1)
    nblk = pl.num_programs(1)
    pad_ref[0:POOL_HALO, :] = jnp.where(i > 0, up_ref[...], 0.0)
    pad_ref[POOL_HALO:POOL_HALO + tm, :] = u_ref[...]
    pad_ref[POOL_HALO + tm:, :] = jnp.where(i < nblk - 1, un_ref[...], 0.0)

    t = i * tm + lax.broadcasted_iota(jnp.int32, (tm, 1), 0)
    n_pad = tm + 2 * POOL_HALO
    outs = []
    for g, w in enumerate(POOL_WINDOWS):
        lo_off = -(w // 2)
        hi_off = w - 1 - w // 2
        c0 = g * gin
        win = pad_ref[:, c0:c0 + gin]
        win = win + pltpu.roll(win, 1, 0)
        width = 2
        while width < w:
            half = width // 2
            win = pltpu.roll(win, n_pad - half, 0) + pltpu.roll(win, half, 0)
            width *= 2
        tot = win[POOL_HALO:POOL_HALO + tm]
        lo = jnp.maximum(t + lo_off, 0)
        hi = jnp.minimum(t + hi_off, seq - 1)
        cnt = (hi - lo + 1).astype(F32)
        pooled = tot / cnt - u_ref[:, c0:c0 + gin]
        outs.append(jnp.dot(pooled.astype(BF16), wp_ref[g], preferred_element_type=F32))
    p = jnp.concatenate(outs, axis=-1) * ps_ref[...]
    m = gt_ref[:, :d_model] * a_ref[...] + gt_ref[:, d_model:] * p
    y = jnp.dot(m.astype(BF16), wo_ref[...], preferred_element_type=F32)
    out = _layer_norm(alpha * x_ref[...] + y, g_ref[...], b_ref[...])
    o32_ref[...] = out
    o16_ref[...] = out.astype(BF16)


def _mix(a, gates, u, x, w_pool, pool_scale, w_o, ln_g, ln_b, *, layer, alpha, tm_pref=256):
    B, S, D = x.shape
    U = u.shape[-1]
    tm = _tile(S, tm_pref)
    hb = tm // POOL_HALO
    n_halo_blocks = S // POOL_HALO
    kernel = functools.partial(_mix_kernel, alpha=alpha, seq=S)
    row = lambda b, i: (b, i, 0)
    vec = pl.BlockSpec((None, 1, D), lambda b, i: (layer, 0, 0))
    return pl.pallas_call(
        kernel,
        grid=(B, S // tm),
        in_specs=[
            pl.BlockSpec((None, tm, D), row),
            pl.BlockSpec((None, tm, 2 * D), row),
            pl.BlockSpec((None, tm, U), row),
            pl.BlockSpec((None, POOL_HALO, U),
                         lambda b, i: (b, jnp.maximum(i * hb - 1, 0), 0)),
            pl.BlockSpec((None, POOL_HALO, U),
                         lambda b, i: (b, jnp.minimum((i + 1) * hb, n_halo_blocks - 1), 0)),
            pl.BlockSpec((None, tm, D), row),
            pl.BlockSpec((None,) + w_pool.shape[1:], lambda b, i: (layer, 0, 0, 0)),
            vec,
            pl.BlockSpec((None, D, D), lambda b, i: (layer, 0, 0)),
            vec,
            vec,
        ],
        out_specs=[pl.BlockSpec((None, tm, D), row), pl.BlockSpec((None, tm, D), row)],
        out_shape=[jax.ShapeDtypeStruct((B, S, D), F32), jax.ShapeDtypeStruct((B, S, D), BF16)],
        scratch_shapes=[pltpu.VMEM((tm + 2 * POOL_HALO, U), F32)],
        compiler_params=_params(("parallel", "arbitrary")),
        name="mix_wo_ln",
    )(a, gates, u, u, u, x, w_pool, pool_scale, w_o, ln_g, ln_b)


def _mlp_kernel(xb_ref, x_ref, wu_ref, wd_ref, g_ref, b_ref, o32_ref, o16_ref, acc_ref, *, alpha):
    f = pl.program_id(2)

    @pl.when(f == 0)
    def _():
        acc_ref[...] = jnp.zeros(acc_ref.shape, F32)

    h = jnp.dot(xb_ref[...], wu_ref[...], preferred_element_type=F32)
    h = jnp.square(jnp.maximum(h, 0.0))
    acc_ref[...] += jnp.dot(h.astype(BF16), wd_ref[...], preferred_element_type=F32)

    @pl.when(f == pl.num_programs(2) - 1)
    def _():
        out = _layer_norm(alpha * x_ref[...] + acc_ref[...], g_ref[...], b_ref[...])
        o32_ref[...] = out
        o16_ref[...] = out.astype(BF16)


def _mlp(xb, x, w_up, w_down, ln_g, ln_b, *, layer, alpha, tm_pref=512, tf_pref=1024):
    B, S, D = x.shape
    F = w_up.shape[2]
    tm = _tile(S, tm_pref)
    tf = _tile(F, tf_pref)
    kernel = functools.partial(_mlp_kernel, alpha=alpha)
    row = lambda b, i, f: (b, i, 0)
    vec = pl.BlockSpec((None, 1, D), lambda b, i, f: (layer, 0, 0))
    return pl.pallas_call(
        kernel,
        grid=(B, S // tm, F // tf),
        in_specs=[
            pl.BlockSpec((None, tm, D), row),
            pl.BlockSpec((None, tm, D), row),
            pl.BlockSpec((None, D, tf), lambda b, i, f: (layer, 0, f)),
            pl.BlockSpec((None, tf, D), lambda b, i, f: (layer, f, 0)),
            vec,
            vec,
        ],
        out_specs=[pl.BlockSpec((None, tm, D), row), pl.BlockSpec((None, tm, D), row)],
        out_shape=[jax.ShapeDtypeStruct((B, S, D), F32), jax.ShapeDtypeStruct((B, S, D), BF16)],
        scratch_shapes=[pltpu.VMEM((tm, D), F32)],
        compiler_params=_params(("parallel", "parallel", "arbitrary")),
        name="mlp_ln",
    )(xb, x, w_up, w_down, ln_g, ln_b)


def _trunk(x, w):
    B, S, D = x.shape
    depth = w["w_in"].shape[0]
    alpha = (2 * depth) ** 0.25
    n_heads = D // HEAD_DIM
    kv_w = N_KV_HEADS * HEAD_DIM
    cos, sin = _rope_tables(S)
    xb = x
    for l in range(depth):
        qk, vT, u, gates = _proj(xb, w["w_in"], w["qk_gains"], cos, sin, layer=l,
                                 qk_w=D + kv_w, v_w=kv_w, u_w=D // 2, tn=PROJ_TN)
        a = _flash(qk, vT, n_heads=n_heads)
        x, xb = _mix(a, gates, u, x, w["w_pool"], w["pool_scale"], w["w_o"],
                     w["ln1_g"], w["ln1_b"], layer=l, alpha=alpha)
        x, xb = _mlp(xb, x, w["w_up"], w["w_down"], w["ln2_g"], w["ln2_b"], layer=l, alpha=alpha)
    return x


def kernel(x_prompt, x_sample, w_in, q_norm, k_norm, w_pool, pool_scale, w_o,
           ln1_g, ln1_b, w_up, w_down, ln2_g, ln2_b):
    depth, d_model, _ = w_in.shape
    kv_w = N_KV_HEADS * HEAD_DIM
    qk_w = d_model + kv_w
    w_in_b = jnp.concatenate([_permute_head_lanes(w_in[:, :, :qk_w]), w_in[:, :, qk_w:]],
                             axis=-1).astype(BF16)
    gq = _permute_head_lanes(q_norm) * SOFTMAX_SCALE_LOG2
    gk = _permute_head_lanes(k_norm)
    qk_gains = jnp.concatenate(
        [jnp.broadcast_to(gq[:, None, None, :], (depth, d_model // PROJ_TN, 1, HEAD_DIM)),
         jnp.broadcast_to(gk[:, None, None, :], (depth, kv_w // PROJ_TN, 1, HEAD_DIM))], axis=1)
    row = lambda v: v.reshape(depth, 1, v.shape[-1])
    w = {
        "w_in": w_in_b, "qk_gains": qk_gains,
        "w_pool": w_pool.astype(BF16), "pool_scale": row(pool_scale),
        "w_o": w_o.astype(BF16),
        "ln1_g": row(ln1_g), "ln1_b": row(ln1_b),
        "w_up": w_up.astype(BF16), "w_down": w_down.astype(BF16),
        "ln2_g": row(ln2_g), "ln2_b": row(ln2_b),
    }
    return (_trunk(x_prompt, w), _trunk(x_sample, w))
```

```python
import functools

import jax
import jax.numpy as jnp
from jax import lax
from jax.experimental import pallas as pl
from jax.experimental.pallas import tpu as pltpu

F32 = jnp.float32
BF16 = jnp.bfloat16

HEAD_DIM = 128
N_KV_HEADS = 4
GRID_W = 64
ROPE_THETA = 10000.0
POOL_WINDOWS = (2, 4, 8, 16)
POOL_HALO = 8
assert all(w >= 2 and w & (w - 1) == 0 and w // 2 <= POOL_HALO for w in POOL_WINDOWS)
LN_EPS = 1e-5
QK_EPS = 1e-6
LOG2E = 1.4426950408889634
SOFTMAX_SCALE_LOG2 = HEAD_DIM ** -0.5 * LOG2E
NEG_BIG = -1e30
ROPE_ROW_BLOCK = 256
PROJ_TN = 512

VMEM_LIMIT_BYTES = 56 * 1024 * 1024


def _tile(n, pref):
    t = min(n, pref)
    assert n % t == 0, (n, pref)
    return t


def _params(sem):
    return pltpu.CompilerParams(dimension_semantics=sem,
                                vmem_limit_bytes=VMEM_LIMIT_BYTES)


def _permute_head_lanes(a):
    lead = a.shape[:-1]
    n = a.shape[-1] // HEAD_DIM
    a = a.reshape(lead + (n, 2, 2, HEAD_DIM // 4))
    a = jnp.swapaxes(a, -3, -2)
    return a.reshape(lead + (n * HEAD_DIM,))


def _rope_tables(seq):
    t = jnp.arange(seq)
    row = (t // GRID_W).astype(F32)
    col = (t % GRID_W).astype(F32)
    axis_dim = HEAD_DIM // 2
    inv_freq = ROPE_THETA ** (-jnp.arange(0, axis_dim, 2, dtype=F32) / axis_dim)
    ar = row[:, None] * inv_freq[None, :]
    ac = col[:, None] * inv_freq[None, :]
    cos = jnp.concatenate([jnp.cos(ar), jnp.cos(ac), jnp.cos(ar), jnp.cos(ac)], axis=-1)
    sin = jnp.concatenate([-jnp.sin(ar), -jnp.sin(ac), jnp.sin(ar), jnp.sin(ac)], axis=-1)
    return cos, sin


def _proj_kernel(x_ref, w_ref, g_ref, cos_ref, sin_ref, qk_ref, vT_ref, u_ref, gt_ref, acc_ref,
                 *maybe_xb_ref, rb, j_v, j_u, j_gate):
    tm, tn = acc_ref.shape
    j = pl.program_id(2)
    heads = [slice(h * HEAD_DIM, (h + 1) * HEAD_DIM) for h in range(tn // HEAD_DIM)]
    row_blocks = [slice(r * rb, (r + 1) * rb) for r in range(tm // rb)]

    lhs_ref = x_ref
    if maybe_xb_ref:
        (lhs_ref,) = maybe_xb_ref

        @pl.when(j == 0)
        def _():
            lhs_ref[...] = x_ref[...].astype(lhs_ref.dtype)

    def block_matmul(rows):
        acc_ref[rows, :] = jnp.dot(lhs_ref[rows, :], w_ref[...], preferred_element_type=F32)

    def interleaved(epilogue):
        block_matmul(row_blocks[0])
        for r, rows in enumerate(row_blocks):
            if r + 1 < len(row_blocks):
                block_matmul(row_blocks[r + 1])
            epilogue(rows)

    def rope_epilogue(rows):
        gain = g_ref[...]
        cos = cos_ref[rows, :]
        sin = sin_ref[rows, :]
        for cols in heads:
            xh = acc_ref[rows, cols]
            ms = jnp.mean(xh * xh, axis=-1, keepdims=True)
            y = xh * lax.rsqrt(ms + QK_EPS) * gain
            rot = y * cos + pltpu.roll(y, HEAD_DIM // 2, 1) * sin
            qk_ref[rows, cols] = rot.astype(qk_ref.dtype)

    def v_epilogue(rows):
        for h, cols in enumerate(heads):
            vT_ref[h, :, rows] = acc_ref[rows, cols].T.astype(vT_ref.dtype)

    def gate_epilogue(rows):
        gt_ref[rows, :] = 1.0 / (1.0 + jnp.exp(-acc_ref[rows, :]))

    pl.when(j < j_v)(lambda: interleaved(rope_epilogue))
    pl.when(j == j_v)(lambda: interleaved(v_epilogue))
    pl.when(j >= j_gate)(lambda: interleaved(gate_epilogue))

    @pl.when((j >= j_u) & (j < j_gate))
    def _():
        u_ref[...] = jnp.dot(lhs_ref[...], w_ref[...], preferred_element_type=F32)


def _proj(x, w, gains, cos, sin, *, layer, qk_w, v_w, u_w, tm_pref=1024, tn=512):
    B, S, D = x.shape
    N = w.shape[2]
    assert v_w == tn and qk_w % tn == 0 and u_w % tn == 0 and N % tn == 0
    tm = _tile(S, tm_pref)
    rb = _tile(tm, ROPE_ROW_BLOCK)
    j_v = qk_w // tn
    j_u = j_v + 1
    j_gate = j_u + u_w // tn
    gate_w = N - j_gate * tn
    kernel = functools.partial(_proj_kernel, rb=rb, j_v=j_v, j_u=j_u, j_gate=j_gate)
    clamp = lambda v, lo, hi: jnp.minimum(jnp.maximum(v, lo), hi)
    return pl.pallas_call(
        kernel,
        grid=(B, S // tm, N // tn),
        in_specs=[
            pl.BlockSpec((None, tm, D), lambda b, i, j: (b, i, 0)),
            pl.BlockSpec((None, D, tn), lambda b, i, j: (layer, 0, j)),
            pl.BlockSpec((None, None, 1, HEAD_DIM),
                         lambda b, i, j: (layer, jnp.minimum(j, j_v - 1), 0, 0)),
            pl.BlockSpec((tm, HEAD_DIM), lambda b, i, j: (i, 0)),
            pl.BlockSpec((tm, HEAD_DIM), lambda b, i, j: (i, 0)),
        ],
        out_specs=[
            pl.BlockSpec((None, tm, tn), lambda b, i, j: (b, i, jnp.minimum(j, j_v - 1))),
            pl.BlockSpec((None, tn // HEAD_DIM, HEAD_DIM, tm), lambda b, i, j: (b, 0, 0, i)),
            pl.BlockSpec((None, tm, tn), lambda b, i, j: (b, i, clamp(j - j_u, 0, j_gate - j_u - 1))),
            pl.BlockSpec((None, tm, tn), lambda b, i, j: (b, i, jnp.maximum(j - j_gate, 0))),
        ],
        out_shape=[
            jax.ShapeDtypeStruct((B, S, qk_w), BF16),
            jax.ShapeDtypeStruct((B, v_w // HEAD_DIM, HEAD_DIM, S), BF16),
            jax.ShapeDtypeStruct((B, S, u_w), F32),
            jax.ShapeDtypeStruct((B, S, gate_w), F32),
        ],
        scratch_shapes=[pltpu.VMEM((tm, tn), F32)]
        + ([] if x.dtype == BF16 else [pltpu.VMEM((tm, D), BF16)]),
        compiler_params=_params(("parallel", "parallel", "arbitrary")),
        name="proj",
    )(x, w, gains, cos, sin)


def _flash_kernel(q_ref, qn_ref, k_ref, kn_ref, vT_ref, o_ref,
                  qT_ref, qTn_ref, m_ref, l_ref, acc_ref, s_ref, p_ref, mx_ref,
                  *, tk, ts, n_chunks, gh):
    G = qT_ref.shape[0]
    tq = qT_ref.shape[2]
    nt = tk // ts

    def slot_window(g):
        sb, h = divmod(g, gh)
        return slice(sb * tq, (sb + 1) * tq), slice(h * HEAD_DIM, (h + 1) * HEAD_DIM)

    def transpose_heads(src_ref, dst_ref):
        for g in range(G):
            rows, cols = slot_window(g)
            dst_ref[g] = src_ref[rows, cols].astype(F32).T.astype(BF16)

    def sub8(x):
        return x.reshape(ts // 8, 8, tq)

    def qk_tile(qt_ref, keys_ref, off, g, t):
        kt = keys_ref[pl.ds(off + t * ts, ts), :]
        s = jnp.dot(kt, qt_ref[g], preferred_element_type=F32)
        s_ref[g, t * ts:(t + 1) * ts, :] = s
        return jnp.max(sub8(s), axis=0)

    def exp_tile(g, t, m_new):
        p = jnp.exp2(s_ref[g, t * ts:(t + 1) * ts, :] - m_new)
        p_ref[g, t * ts:(t + 1) * ts, :] = p.astype(BF16)
        return jnp.sum(sub8(p), axis=0)

    def chunk_step(off, qt_next_ref, keys_next_ref, off_next):
        for g in range(G):
            m_old = m_ref[g]
            m_new = jnp.maximum(m_old, jnp.max(mx_ref[g], axis=0, keepdims=True))
            alpha = jnp.exp2(m_old - m_new)
            m_ref[g] = m_new
            lsum = None
            mx = None
            for t in range(nt):
                ls = exp_tile(g, t, m_new)
                lsum = ls if lsum is None else lsum + ls
                tile_mx = qk_tile(qt_next_ref, keys_next_ref, off_next, g, t)
                mx = tile_mx if mx is None else jnp.maximum(mx, tile_mx)
            l_ref[g] = alpha * l_ref[g] + lsum
            acc_ref[g] = alpha * acc_ref[g] + jnp.dot(
                vT_ref[:, pl.ds(off, tk)], p_ref[g], preferred_element_type=F32)
            mx_ref[g] = mx

    m_ref[...] = jnp.full(m_ref.shape, NEG_BIG, F32)
    l_ref[...] = jnp.zeros(l_ref.shape, F32)
    acc_ref[...] = jnp.zeros(acc_ref.shape, F32)

    @pl.when((pl.program_id(0) == 0) & (pl.program_id(1) == 0) & (pl.program_id(2) == 0))
    def _():
        transpose_heads(q_ref, qT_ref)
        for g in range(G):
            mx = None
            for t in range(nt):
                tile_mx = qk_tile(qT_ref, k_ref, 0, g, t)
                mx = tile_mx if mx is None else jnp.maximum(mx, tile_mx)
            mx_ref[g] = mx

    transpose_heads(qn_ref, qTn_ref)

    def body(c, carry):
        off = pl.multiple_of(c * tk, tk)
        chunk_step(off, qT_ref, k_ref, pl.multiple_of(off + tk, tk))
        return carry

    lax.fori_loop(0, n_chunks - 1, body, 0)
    chunk_step((n_chunks - 1) * tk, qTn_ref, kn_ref, 0)

    for g in range(G):
        o = acc_ref[g] / jnp.sum(l_ref[g], axis=0, keepdims=True)
        rows, cols = slot_window(g)
        o_ref[rows, cols] = o.T.astype(o_ref.dtype)
    qT_ref[...] = qTn_ref[...]


def _flash(qk, vT, *, n_heads, tq_pref=256, nsub_pref=4, tk_pref=1024, ts_pref=128):
    B, KV, hd, S = vT.shape
    gh = n_heads // KV
    tq = _tile(S, tq_pref)
    nsub = _tile(S // tq, nsub_pref)
    tqb = nsub * tq
    G = nsub * gh
    tk = _tile(S, tk_pref)
    ts = _tile(tk, ts_pref)
    nq = S // tqb
    kernel = functools.partial(_flash_kernel, tk=tk, ts=ts, n_chunks=S // tk, gh=gh)

    def next_step(b, h, i):
        wrap = i == nq - 1
        flat = jnp.minimum(b * KV + h + wrap.astype(jnp.int32), B * KV - 1)
        last = (b * KV + h == B * KV - 1) & wrap
        return flat // KV, flat % KV, jnp.where(wrap & ~last, 0, jnp.minimum(i + 1, nq - 1))

    def next_q(b, h, i):
        nb, nh, ni = next_step(b, h, i)
        return nb, ni, nh

    def next_k(b, h, i):
        nb, nh, _ = next_step(b, h, i)
        return nb, 0, n_heads + nh

    return pl.pallas_call(
        kernel,
        grid=(B, KV, nq),
        in_specs=[
            pl.BlockSpec((None, tqb, gh * hd), lambda b, h, i: (b, i, h)),
            pl.BlockSpec((None, tqb, gh * hd), next_q),
            pl.BlockSpec((None, S, hd), lambda b, h, i: (b, 0, n_heads + h)),
            pl.BlockSpec((None, tk, hd), next_k),
            pl.BlockSpec((None, None, hd, S), lambda b, h, i: (b, h, 0, 0)),
        ],
        out_specs=pl.BlockSpec((None, tqb, gh * hd), lambda b, h, i: (b, i, h)),
        out_shape=jax.ShapeDtypeStruct((B, S, n_heads * hd), F32),
        scratch_shapes=[
            pltpu.VMEM((G, hd, tq), BF16),
            pltpu.VMEM((G, hd, tq), BF16),
            pltpu.VMEM((G, 1, tq), F32),
            pltpu.VMEM((G, 8, tq), F32),
            pltpu.VMEM((G, hd, tq), F32),
            pltpu.VMEM((G, tk, tq), F32),
            pltpu.VMEM((G, tk, tq), BF16),
            pltpu.VMEM((G, 8, tq), F32),
        ],
        compiler_params=_params(("arbitrary", "arbitrary", "arbitrary")),
        name="flash_attn",
    )(qk, qk, qk, qk, vT)


def _layer_norm(z, g, b):
    mu = jnp.mean(z, axis=-1, keepdims=True)
    zc = z - mu
    var = jnp.mean(zc * zc, axis=-1, keepdims=True)
    return zc * lax.rsqrt(var + LN_EPS) * g + b


def _mix_kernel(a_ref, gt_ref, u_ref, up_ref, un_ref, x_ref, wp_ref, ps_ref, wo_ref,
                g_ref, b_ref, o32_ref, o16_ref, pad_ref, *, alpha, seq):
    tm = u_ref.shape[0]
    d_model = a_ref.shape[1]
    n_groups = len(POOL_WINDOWS)
    gin = u_ref.shape[1] // n_groups
    i = pl.program_id(1)
    nblk = pl.num_programs(1)
    pad_ref[0:POOL_HALO, :] = jnp.where(i > 0, up_ref[...], 0.0)
    pad_ref[POOL_HALO:POOL_HALO + tm, :] = u_ref[...]
    pad_ref[POOL_HALO + tm:, :] = jnp.where(i < nblk - 1, un_ref[...], 0.0)

    t = i * tm + lax.broadcasted_iota(jnp.int32, (tm, 1), 0)
    n_pad = tm + 2 * POOL_HALO
    outs = []
    for g, w in enumerate(POOL_WINDOWS):
        lo_off = -(w // 2)
        hi_off = w - 1 - w // 2
        c0 = g * gin
        win = pad_ref[:, c0:c0 + gin]
        win = win + pltpu.roll(win, 1, 0)
        width = 2
        while width < w:
            half = width // 2
            win = pltpu.roll(win, n_pad - half, 0) + pltpu.roll(win, half, 0)
            width *= 2
        tot = win[POOL_HALO:POOL_HALO + tm]
        lo = jnp.maximum(t + lo_off, 0)
        hi = jnp.minimum(t + hi_off, seq - 1)
        cnt = (hi - lo + 1).astype(F32)
        pooled = tot / cnt - u_ref[:, c0:c0 + gin]
        outs.append(jnp.dot(pooled.astype(BF16), wp_ref[g], preferred_element_type=F32))
    p = jnp.concatenate(outs, axis=-1) * ps_ref[...]
    m = gt_ref[:, :d_model] * a_ref[...] + gt_ref[:, d_model:] * p
    y = jnp.dot(m.astype(BF16), wo_ref[...], preferred_element_type=F32)
    out = _layer_norm(alpha * x_ref[...] + y, g_ref[...], b_ref[...])
    o32_ref[...] = out
    o16_ref[...] = out.astype(BF16)


def _mix(a, gates, u, x, w_pool, pool_scale, w_o, ln_g, ln_b, *, layer, alpha, tm_pref=256):
    B, S, D = x.shape
    U = u.shape[-1]
    tm = _tile(S, tm_pref)
    hb = tm // POOL_HALO
    n_halo_blocks = S // POOL_HALO
    kernel = functools.partial(_mix_kernel, alpha=alpha, seq=S)
    row = lambda b, i: (b, i, 0)
    vec = pl.BlockSpec((None, 1, D), lambda b, i: (layer, 0, 0))
    return pl.pallas_call(
        kernel,
        grid=(B, S // tm),
        in_specs=[
            pl.BlockSpec((None, tm, D), row),
            pl.BlockSpec((None, tm, 2 * D), row),
            pl.BlockSpec((None, tm, U), row),
            pl.BlockSpec((None, POOL_HALO, U),
                         lambda b, i: (b, jnp.maximum(i * hb - 1, 0), 0)),
            pl.BlockSpec((None, POOL_HALO, U),
                         lambda b, i: (b, jnp.minimum((i + 1) * hb, n_halo_blocks - 1), 0)),
            pl.BlockSpec((None, tm, D), row),
            pl.BlockSpec((None,) + w_pool.shape[1:], lambda b, i: (layer, 0, 0, 0)),
            vec,
            pl.BlockSpec((None, D, D), lambda b, i: (layer, 0, 0)),
            vec,
            vec,
        ],
        out_specs=[pl.BlockSpec((None, tm, D), row), pl.BlockSpec((None, tm, D), row)],
        out_shape=[jax.ShapeDtypeStruct((B, S, D), F32), jax.ShapeDtypeStruct((B, S, D), BF16)],
        scratch_shapes=[pltpu.VMEM((tm + 2 * POOL_HALO, U), F32)],
        compiler_params=_params(("parallel", "arbitrary")),
        name="mix_wo_ln",
    )(a, gates, u, u, u, x, w_pool, pool_scale, w_o, ln_g, ln_b)


def _mlp_kernel(xb_ref, x_ref, wu_ref, wd_ref, g_ref, b_ref, o32_ref, o16_ref, acc_ref, *, alpha):
    r = pl.program_id(0)
    f = pl.program_id(1)
    n_rows = pl.num_programs(0) - 1

    def down_of_up():
        h = jnp.dot(xb_ref[...], wu_ref[...], preferred_element_type=F32)
        h = jnp.square(jnp.maximum(h, 0.0))
        return jnp.dot(h.astype(BF16), wd_ref[...], preferred_element_type=F32)

    def norm_previous():
        out = _layer_norm(alpha * x_ref[...] + acc_ref[...], g_ref[...], b_ref[...])
        o32_ref[...] = out
        o16_ref[...] = out.astype(BF16)

    @pl.when((f == 0) & (r == 0))
    def _():
        acc_ref[...] = down_of_up()

    @pl.when((f == 0) & (r > 0) & (r < n_rows))
    def _():
        norm_previous()
        acc_ref[...] = down_of_up()

    @pl.when((f == 0) & (r == n_rows))
    def _():
        norm_previous()

    @pl.when((f > 0) & (r < n_rows))
    def _():
        acc_ref[...] += down_of_up()


def _mlp(xb, x, w_up, w_down, ln_g, ln_b, *, layer, alpha, tm_pref=512, tf_pref=1024):
    B, S, D = x.shape
    F = w_up.shape[2]
    tm = _tile(S, tm_pref)
    tf = _tile(F, tf_pref)
    n_rows = B * S // tm
    kernel = functools.partial(_mlp_kernel, alpha=alpha)
    summed = lambda r, f: (jnp.minimum(r, n_rows - 1), 0)
    normed = lambda r, f: (jnp.maximum(r - 1, 0), 0)
    vec = pl.BlockSpec((None, 1, D), lambda r, f: (layer, 0, 0))
    out32, out16 = pl.pallas_call(
        kernel,
        grid=(n_rows + 1, F // tf),
        in_specs=[
            pl.BlockSpec((tm, D), summed),
            pl.BlockSpec((tm, D), normed),
            pl.BlockSpec((None, D, tf), lambda r, f: (layer, 0, f)),
            pl.BlockSpec((None, tf, D), lambda r, f: (layer, f, 0)),
            vec,
            vec,
        ],
        out_specs=[pl.BlockSpec((tm, D), normed), pl.BlockSpec((tm, D), normed)],
        out_shape=[jax.ShapeDtypeStruct((B * S, D), F32), jax.ShapeDtypeStruct((B * S, D), BF16)],
        scratch_shapes=[pltpu.VMEM((tm, D), F32)],
        compiler_params=_params(("arbitrary", "arbitrary")),
        name="mlp_ln",
    )(xb.reshape(B * S, D), x.reshape(B * S, D), w_up, w_down, ln_g, ln_b)
    return out32.reshape(B, S, D), out16.reshape(B, S, D)


def _trunk(x, w):
    B, S, D = x.shape
    depth = w["w_in"].shape[0]
    alpha = (2 * depth) ** 0.25
    n_heads = D // HEAD_DIM
    kv_w = N_KV_HEADS * HEAD_DIM
    cos, sin = _rope_tables(S)
    xb = x
    for l in range(depth):
        qk, vT, u, gates = _proj(xb, w["w_in"], w["qk_gains"], cos, sin, layer=l,
                                 qk_w=D + kv_w, v_w=kv_w, u_w=D // 2, tn=PROJ_TN)
        a = _flash(qk, vT, n_heads=n_heads)
        x, xb = _mix(a, gates, u, x, w["w_pool"], w["pool_scale"], w["w_o"],
                     w["ln1_g"], w["ln1_b"], layer=l, alpha=alpha)
        x, xb = _mlp(xb, x, w["w_up"], w["w_down"], w["ln2_g"], w["ln2_b"], layer=l, alpha=alpha)
    return x


def kernel(x_prompt, x_sample, w_in, q_norm, k_norm, w_pool, pool_scale, w_o,
           ln1_g, ln1_b, w_up, w_down, ln2_g, ln2_b):
    depth, d_model, _ = w_in.shape
    kv_w = N_KV_HEADS * HEAD_DIM
    qk_w = d_model + kv_w
    w_in_b = jnp.concatenate([_permute_head_lanes(w_in[:, :, :qk_w]), w_in[:, :, qk_w:]],
                             axis=-1).astype(BF16)
    gq = _permute_head_lanes(q_norm) * SOFTMAX_SCALE_LOG2
    gk = _permute_head_lanes(k_norm)
    qk_gains = jnp.concatenate(
        [jnp.broadcast_to(gq[:, None, None, :], (depth, d_model // PROJ_TN, 1, HEAD_DIM)),
         jnp.broadcast_to(gk[:, None, None, :], (depth, kv_w // PROJ_TN, 1, HEAD_DIM))], axis=1)
    row = lambda v: v.reshape(depth, 1, v.shape[-1])
    w = {
        "w_in": w_in_b, "qk_gains": qk_gains,
        "w_pool": w_pool.astype(BF16), "pool_scale": row(pool_scale),
        "w_o": w_o.astype(BF16),
        "ln1_g": row(ln1_g), "ln1_b": row(ln1_b),
        "w_up": w_up.astype(BF16), "w_down": w_down.astype(BF16),
        "ln2_g": row(ln2_g), "ln2_b": row(ln2_b),
    }
    return (_trunk(x_prompt, w), _trunk(x_sample, w))
```

```python
import functools

import jax
import jax.numpy as jnp
from jax import lax
from jax.experimental import pallas as pl
from jax.experimental.pallas import tpu as pltpu

F32 = jnp.float32
BF16 = jnp.bfloat16

HEAD_DIM = 128
N_KV_HEADS = 4
GRID_W = 64
ROPE_THETA = 10000.0
POOL_WINDOWS = (2, 4, 8, 16)
POOL_HALO = 8
assert all(w >= 2 and w & (w - 1) == 0 and w // 2 <= POOL_HALO for w in POOL_WINDOWS)
LN_EPS = 1e-5
QK_EPS = 1e-6
LOG2E = 1.4426950408889634
SOFTMAX_SCALE_LOG2 = HEAD_DIM ** -0.5 * LOG2E
NEG_BIG = -1e30
ROPE_ROW_BLOCK = 128
PROJ_TN = 512

VMEM_LIMIT_BYTES = 56 * 1024 * 1024


def _tile(n, pref):
    t = min(n, pref)
    assert n % t == 0, (n, pref)
    return t


def _params(sem):
    return pltpu.CompilerParams(dimension_semantics=sem,
                                vmem_limit_bytes=VMEM_LIMIT_BYTES)


def _permute_head_lanes(a):
    lead = a.shape[:-1]
    n = a.shape[-1] // HEAD_DIM
    a = a.reshape(lead + (n, 2, 2, HEAD_DIM // 4))
    a = jnp.swapaxes(a, -3, -2)
    return a.reshape(lead + (n * HEAD_DIM,))


def _rope_tables(seq):
    t = jnp.arange(seq)
    row = (t // GRID_W).astype(F32)
    col = (t % GRID_W).astype(F32)
    axis_dim = HEAD_DIM // 2
    inv_freq = ROPE_THETA ** (-jnp.arange(0, axis_dim, 2, dtype=F32) / axis_dim)
    ar = row[:, None] * inv_freq[None, :]
    ac = col[:, None] * inv_freq[None, :]
    cos = jnp.concatenate([jnp.cos(ar), jnp.cos(ac), jnp.cos(ar), jnp.cos(ac)], axis=-1)
    sin = jnp.concatenate([-jnp.sin(ar), -jnp.sin(ac), jnp.sin(ar), jnp.sin(ac)], axis=-1)
    return cos, sin


def _proj_kernel(x_ref, w_ref, g_ref, cos_ref, sin_ref, qk_ref, vT_ref, u_ref, gt_ref, acc_ref,
                 *maybe_xb_ref, rb, j_v, j_u, j_gate):
    tm, tn = acc_ref.shape
    j = pl.program_id(2)
    heads = [slice(h * HEAD_DIM, (h + 1) * HEAD_DIM) for h in range(tn // HEAD_DIM)]
    row_blocks = [slice(r * rb, (r + 1) * rb) for r in range(tm // rb)]

    lhs_ref = x_ref
    if maybe_xb_ref:
        (lhs_ref,) = maybe_xb_ref

        @pl.when(j == 0)
        def _():
            lhs_ref[...] = x_ref[...].astype(lhs_ref.dtype)

    def block_matmul(rows):
        acc_ref[rows, :] = jnp.dot(lhs_ref[rows, :], w_ref[...], preferred_element_type=F32)

    def interleaved(epilogue):
        block_matmul(row_blocks[0])
        for r, rows in enumerate(row_blocks):
            if r + 1 < len(row_blocks):
                block_matmul(row_blocks[r + 1])
            epilogue(rows)

    def rope_epilogue(rows):
        gain = g_ref[...]
        cos = cos_ref[rows, :]
        sin = sin_ref[rows, :]
        for cols in heads:
            xh = acc_ref[rows, cols]
            ms = jnp.mean(xh * xh, axis=-1, keepdims=True)
            y = xh * lax.rsqrt(ms + QK_EPS) * gain
            rot = y * cos + pltpu.roll(y, HEAD_DIM // 2, 1) * sin
            qk_ref[rows, cols] = rot.astype(qk_ref.dtype)

    def v_epilogue(rows):
        for h, cols in enumerate(heads):
            vT_ref[h, :, rows] = acc_ref[rows, cols].T.astype(vT_ref.dtype)

    def gate_epilogue(rows):
        gt_ref[rows, :] = 1.0 / (1.0 + jnp.exp(-acc_ref[rows, :]))

    pl.when(j < j_v)(lambda: interleaved(rope_epilogue))
    pl.when(j == j_v)(lambda: interleaved(v_epilogue))
    pl.when(j >= j_gate)(lambda: interleaved(gate_epilogue))

    @pl.when((j >= j_u) & (j < j_gate))
    def _():
        u_ref[...] = jnp.dot(lhs_ref[...], w_ref[...], preferred_element_type=F32)


def _proj(x, w, gains, cos, sin, *, layer, qk_w, v_w, u_w, tm_pref=1024, tn=512):
    B, S, D = x.shape
    N = w.shape[2]
    assert v_w == tn and qk_w % tn == 0 and u_w % tn == 0 and N % tn == 0
    tm = _tile(S, tm_pref)
    rb = _tile(tm, ROPE_ROW_BLOCK)
    j_v = qk_w // tn
    j_u = j_v + 1
    j_gate = j_u + u_w // tn
    gate_w = N - j_gate * tn
    kernel = functools.partial(_proj_kernel, rb=rb, j_v=j_v, j_u=j_u, j_gate=j_gate)
    clamp = lambda v, lo, hi: jnp.minimum(jnp.maximum(v, lo), hi)
    return pl.pallas_call(
        kernel,
        grid=(B, S // tm, N // tn),
        in_specs=[
            pl.BlockSpec((None, tm, D), lambda b, i, j: (b, i, 0)),
            pl.BlockSpec((None, D, tn), lambda b, i, j: (layer, 0, j)),
            pl.BlockSpec((None, None, 1, HEAD_DIM),
                         lambda b, i, j: (layer, jnp.minimum(j, j_v - 1), 0, 0)),
            pl.BlockSpec((tm, HEAD_DIM), lambda b, i, j: (i, 0)),
            pl.BlockSpec((tm, HEAD_DIM), lambda b, i, j: (i, 0)),
        ],
        out_specs=[
            pl.BlockSpec((None, tm, tn), lambda b, i, j: (b, i, jnp.minimum(j, j_v - 1))),
            pl.BlockSpec((None, tn // HEAD_DIM, HEAD_DIM, tm), lambda b, i, j: (b, 0, 0, i)),
            pl.BlockSpec((None, tm, tn), lambda b, i, j: (b, i, clamp(j - j_u, 0, j_gate - j_u - 1))),
            pl.BlockSpec((None, tm, tn), lambda b, i, j: (b, i, jnp.maximum(j - j_gate, 0))),
        ],
        out_shape=[
            jax.ShapeDtypeStruct((B, S, qk_w), BF16),
            jax.ShapeDtypeStruct((B, v_w // HEAD_DIM, HEAD_DIM, S), BF16),
            jax.ShapeDtypeStruct((B, S, u_w), F32),
            jax.ShapeDtypeStruct((B, S, gate_w), F32),
        ],
        scratch_shapes=[pltpu.VMEM((tm, tn), F32)]
        + ([] if x.dtype == BF16 else [pltpu.VMEM((tm, D), BF16)]),
        compiler_params=_params(("parallel", "parallel", "arbitrary")),
        name="proj",
    )(x, w, gains, cos, sin)


def _flash_kernel(q_ref, qn_ref, k_ref, kn_ref, vT_ref, o_ref,
                  qT_ref, qTn_ref, m_ref, l_ref, acc_ref, s_ref, p_ref, mx_ref,
                  *, tk, ts, n_chunks, gh):
    G = qT_ref.shape[0]
    tq = qT_ref.shape[2]
    nt = tk // ts

    def slot_window(g):
        sb, h = divmod(g, gh)
        return slice(sb * tq, (sb + 1) * tq), slice(h * HEAD_DIM, (h + 1) * HEAD_DIM)

    def transpose_heads(src_ref, dst_ref):
        for g in range(G):
            rows, cols = slot_window(g)
            dst_ref[g] = src_ref[rows, cols].astype(F32).T.astype(BF16)

    def sub8(x):
        return x.reshape(ts // 8, 8, tq)

    def qk_tile(qt_ref, keys_ref, off, g, t):
        kt = keys_ref[pl.ds(off + t * ts, ts), :]
        s = jnp.dot(kt, qt_ref[g], preferred_element_type=F32)
        s_ref[g, t * ts:(t + 1) * ts, :] = s
        return jnp.max(sub8(s), axis=0)

    def exp_tile(g, t, m_new):
        p = jnp.exp2(s_ref[g, t * ts:(t + 1) * ts, :] - m_new)
        p_ref[g, t * ts:(t + 1) * ts, :] = p.astype(BF16)
        return jnp.sum(sub8(p), axis=0)

    def chunk_step(off, qt_next_ref, keys_next_ref, off_next):
        for g in range(G):
            m_old = m_ref[g]
            m_new = jnp.maximum(m_old, jnp.max(mx_ref[g], axis=0, keepdims=True))
            alpha = jnp.exp2(m_old - m_new)
            m_ref[g] = m_new
            lsum = None
            mx = None
            for t in range(nt):
                ls = exp_tile(g, t, m_new)
                lsum = ls if lsum is None else lsum + ls
                tile_mx = qk_tile(qt_next_ref, keys_next_ref, off_next, g, t)
                mx = tile_mx if mx is None else jnp.maximum(mx, tile_mx)
            l_ref[g] = alpha * l_ref[g] + lsum
            acc_ref[g] = alpha * acc_ref[g] + jnp.dot(
                vT_ref[:, pl.ds(off, tk)], p_ref[g], preferred_element_type=F32)
            mx_ref[g] = mx

    m_ref[...] = jnp.full(m_ref.shape, NEG_BIG, F32)
    l_ref[...] = jnp.zeros(l_ref.shape, F32)
    acc_ref[...] = jnp.zeros(acc_ref.shape, F32)

    @pl.when((pl.program_id(0) == 0) & (pl.program_id(1) == 0) & (pl.program_id(2) == 0))
    def _():
        transpose_heads(q_ref, qT_ref)
        for g in range(G):
            mx = None
            for t in range(nt):
                tile_mx = qk_tile(qT_ref, k_ref, 0, g, t)
                mx = tile_mx if mx is None else jnp.maximum(mx, tile_mx)
            mx_ref[g] = mx

    transpose_heads(qn_ref, qTn_ref)

    def body(c, carry):
        off = pl.multiple_of(c * tk, tk)
        chunk_step(off, qT_ref, k_ref, pl.multiple_of(off + tk, tk))
        return carry

    lax.fori_loop(0, n_chunks - 1, body, 0)
    chunk_step((n_chunks - 1) * tk, qTn_ref, kn_ref, 0)

    for g in range(G):
        o = acc_ref[g] / jnp.sum(l_ref[g], axis=0, keepdims=True)
        rows, cols = slot_window(g)
        o_ref[rows, cols] = o.T.astype(o_ref.dtype)
    qT_ref[...] = qTn_ref[...]


def _flash(qk, vT, *, n_heads, tq_pref=256, nsub_pref=4, tk_pref=1024, ts_pref=128):
    B, KV, hd, S = vT.shape
    gh = n_heads // KV
    tq = _tile(S, tq_pref)
    nsub = _tile(S // tq, nsub_pref)
    tqb = nsub * tq
    G = nsub * gh
    tk = _tile(S, tk_pref)
    ts = _tile(tk, ts_pref)
    nq = S // tqb
    kernel = functools.partial(_flash_kernel, tk=tk, ts=ts, n_chunks=S // tk, gh=gh)

    def next_step(b, h, i):
        wrap = i == nq - 1
        flat = jnp.minimum(b * KV + h + wrap.astype(jnp.int32), B * KV - 1)
        last = (b * KV + h == B * KV - 1) & wrap
        return flat // KV, flat % KV, jnp.where(wrap & ~last, 0, jnp.minimum(i + 1, nq - 1))

    def next_q(b, h, i):
        nb, nh, ni = next_step(b, h, i)
        return nb, ni, nh

    def next_k(b, h, i):
        nb, nh, _ = next_step(b, h, i)
        return nb, 0, n_heads + nh

    return pl.pallas_call(
        kernel,
        grid=(B, KV, nq),
        in_specs=[
            pl.BlockSpec((None, tqb, gh * hd), lambda b, h, i: (b, i, h)),
            pl.BlockSpec((None, tqb, gh * hd), next_q),
            pl.BlockSpec((None, S, hd), lambda b, h, i: (b, 0, n_heads + h)),
            pl.BlockSpec((None, tk, hd), next_k),
            pl.BlockSpec((None, None, hd, S), lambda b, h, i: (b, h, 0, 0)),
        ],
        out_specs=pl.BlockSpec((None, tqb, gh * hd), lambda b, h, i: (b, i, h)),
        out_shape=jax.ShapeDtypeStruct((B, S, n_heads * hd), F32),
        scratch_shapes=[
            pltpu.VMEM((G, hd, tq), BF16),
            pltpu.VMEM((G, hd, tq), BF16),
            pltpu.VMEM((G, 1, tq), F32),
            pltpu.VMEM((G, 8, tq), F32),
            pltpu.VMEM((G, hd, tq), F32),
            pltpu.VMEM((G, tk, tq), F32),
            pltpu.VMEM((G, tk, tq), BF16),
            pltpu.VMEM((G, 8, tq), F32),
        ],
        compiler_params=_params(("arbitrary", "arbitrary", "arbitrary")),
        name="flash_attn",
    )(qk, qk, qk, qk, vT)


def _layer_norm(z, g, b):
    mu = jnp.mean(z, axis=-1, keepdims=True)
    zc = z - mu
    var = jnp.mean(zc * zc, axis=-1, keepdims=True)
    return zc * lax.rsqrt(var + LN_EPS) * g + b


def _mix_kernel(a_ref, gt_ref, u_ref, up_ref, un_ref, x_ref, wp_ref, ps_ref, wo_ref,
                g_ref, b_ref, o32_ref, o16_ref, pad_ref, *, alpha, seq):
    tm = u_ref.shape[0]
    d_model = a_ref.shape[1]
    n_groups = len(POOL_WINDOWS)
    gin = u_ref.shape[1] // n_groups
    i = pl.program_id(1)
    nblk = pl.num_programs(1)
    pad_ref[0:POOL_HALO, :] = jnp.where(i > 0, up_ref[...], 0.0)
    pad_ref[POOL_HALO:POOL_HALO + tm, :] = u_ref[...]
    pad_ref[POOL_HALO + tm:, :] = jnp.where(i < nblk - 1, un_ref[...], 0.0)

    t = i * tm + lax.broadcasted_iota(jnp.int32, (tm, 1), 0)
    n_pad = tm + 2 * POOL_HALO
    outs = []
    for g, w in enumerate(POOL_WINDOWS):
        lo_off = -(w // 2)
        hi_off = w - 1 - w // 2
        c0 = g * gin
        win = pad_ref[:, c0:c0 + gin]
        win = win + pltpu.roll(win, 1, 0)
        width = 2
        while width < w:
            half = width // 2
            win = pltpu.roll(win, n_pad - half, 0) + pltpu.roll(win, half, 0)
            width *= 2
        tot = win[POOL_HALO:POOL_HALO + tm]
        lo = jnp.maximum(t + lo_off, 0)
        hi = jnp.minimum(t + hi_off, seq - 1)
        cnt = (hi - lo + 1).astype(F32)
        pooled = tot / cnt - u_ref[:, c0:c0 + gin]
        outs.append(jnp.dot(pooled.astype(BF16), wp_ref[g], preferred_element_type=F32))
    p = jnp.concatenate(outs, axis=-1) * ps_ref[...]
    m = gt_ref[:, :d_model] * a_ref[...] + gt_ref[:, d_model:] * p
    y = jnp.dot(m.astype(BF16), wo_ref[...], preferred_element_type=F32)
    out = _layer_norm(alpha * x_ref[...] + y, g_ref[...], b_ref[...])
    o32_ref[...] = out
    o16_ref[...] = out.astype(BF16)


def _mix(a, gates, u, x, w_pool, pool_scale, w_o, ln_g, ln_b, *, layer, alpha, tm_pref=256):
    B, S, D = x.shape
    U = u.shape[-1]
    tm = _tile(S, tm_pref)
    hb = tm // POOL_HALO
    n_halo_blocks = S // POOL_HALO
    kernel = functools.partial(_mix_kernel, alpha=alpha, seq=S)
    row = lambda b, i: (b, i, 0)
    vec = pl.BlockSpec((None, 1, D), lambda b, i: (layer, 0, 0))
    return pl.pallas_call(
        kernel,
        grid=(B, S // tm),
        in_specs=[
            pl.BlockSpec((None, tm, D), row),
            pl.BlockSpec((None, tm, 2 * D), row),
            pl.BlockSpec((None, tm, U), row),
            pl.BlockSpec((None, POOL_HALO, U),
                         lambda b, i: (b, jnp.maximum(i * hb - 1, 0), 0)),
            pl.BlockSpec((None, POOL_HALO, U),
                         lambda b, i: (b, jnp.minimum((i + 1) * hb, n_halo_blocks - 1), 0)),
            pl.BlockSpec((None, tm, D), row),
            pl.BlockSpec((None,) + w_pool.shape[1:], lambda b, i: (layer, 0, 0, 0)),
            vec,
            pl.BlockSpec((None, D, D), lambda b, i: (layer, 0, 0)),
            vec,
            vec,
        ],
        out_specs=[pl.BlockSpec((None, tm, D), row), pl.BlockSpec((None, tm, D), row)],
        out_shape=[jax.ShapeDtypeStruct((B, S, D), F32), jax.ShapeDtypeStruct((B, S, D), BF16)],
        scratch_shapes=[pltpu.VMEM((tm + 2 * POOL_HALO, U), F32)],
        compiler_params=_params(("parallel", "arbitrary")),
        name="mix_wo_ln",
    )(a, gates, u, u, u, x, w_pool, pool_scale, w_o, ln_g, ln_b)


def _mlp_kernel(xb_ref, x_ref, wu_ref, wd_ref, g_ref, b_ref, o32_ref, o16_ref, acc_ref, *, alpha):
    r = pl.program_id(0)
    f = pl.program_id(1)
    n_rows = pl.num_programs(0) - 1

    def down_of_up():
        h = jnp.dot(xb_ref[...], wu_ref[...], preferred_element_type=F32)
        h = jnp.square(jnp.maximum(h, 0.0))
        return jnp.dot(h.astype(BF16), wd_ref[...], preferred_element_type=F32)

    def norm_previous():
        out = _layer_norm(alpha * x_ref[...] + acc_ref[...], g_ref[...], b_ref[...])
        o32_ref[...] = out
        o16_ref[...] = out.astype(BF16)

    @pl.when((f == 0) & (r == 0))
    def _():
        acc_ref[...] = down_of_up()

    @pl.when((f == 0) & (r > 0) & (r < n_rows))
    def _():
        norm_previous()
        acc_ref[...] = down_of_up()

    @pl.when((f == 0) & (r == n_rows))
    def _():
        norm_previous()

    @pl.when((f > 0) & (r < n_rows))
    def _():
        acc_ref[...] += down_of_up()


def _mlp(xb, x, w_up, w_down, ln_g, ln_b, *, layer, alpha, tm_pref=512, tf_pref=1024):
    B, S, D = x.shape
    F = w_up.shape[2]
    tm = _tile(S, tm_pref)
    tf = _tile(F, tf_pref)
    n_rows = B * S // tm
    kernel = functools.partial(_mlp_kernel, alpha=alpha)
    summed = lambda r, f: (jnp.minimum(r, n_rows - 1), 0)
    normed = lambda r, f: (jnp.maximum(r - 1, 0), 0)
    vec = pl.BlockSpec((None, 1, D), lambda r, f: (layer, 0, 0))
    out32, out16 = pl.pallas_call(
        kernel,
        grid=(n_rows + 1, F // tf),
        in_specs=[
            pl.BlockSpec((tm, D), summed),
            pl.BlockSpec((tm, D), normed),
            pl.BlockSpec((None, D, tf), lambda r, f: (layer, 0, f)),
            pl.BlockSpec((None, tf, D), lambda r, f: (layer, f, 0)),
            vec,
            vec,
        ],
        out_specs=[pl.BlockSpec((tm, D), normed), pl.BlockSpec((tm, D), normed)],
        out_shape=[jax.ShapeDtypeStruct((B * S, D), F32), jax.ShapeDtypeStruct((B * S, D), BF16)],
        scratch_shapes=[pltpu.VMEM((tm, D), F32)],
        compiler_params=_params(("arbitrary", "arbitrary")),
        name="mlp_ln",
    )(xb.reshape(B * S, D), x.reshape(B * S, D), w_up, w_down, ln_g, ln_b)
    return out32.reshape(B, S, D), out16.reshape(B, S, D)


def _trunk(x, w):
    B, S, D = x.shape
    depth = w["w_in"].shape[0]
    alpha = (2 * depth) ** 0.25
    n_heads = D // HEAD_DIM
    kv_w = N_KV_HEADS * HEAD_DIM
    cos, sin = _rope_tables(S)
    xb = x
    for l in range(depth):
        qk, vT, u, gates = _proj(xb, w["w_in"], w["qk_gains"], cos, sin, layer=l,
                                 qk_w=D + kv_w, v_w=kv_w, u_w=D // 2, tn=PROJ_TN)
        a = _flash(qk, vT, n_heads=n_heads)
        x, xb = _mix(a, gates, u, x, w["w_pool"], w["pool_scale"], w["w_o"],
                     w["ln1_g"], w["ln1_b"], layer=l, alpha=alpha)
        x, xb = _mlp(xb, x, w["w_up"], w["w_down"], w["ln2_g"], w["ln2_b"], layer=l, alpha=alpha)
    return x


def kernel(x_prompt, x_sample, w_in, q_norm, k_norm, w_pool, pool_scale, w_o,
           ln1_g, ln1_b, w_up, w_down, ln2_g, ln2_b):
    depth, d_model, _ = w_in.shape
    kv_w = N_KV_HEADS * HEAD_DIM
    qk_w = d_model + kv_w
    w_in_b = jnp.concatenate([_permute_head_lanes(w_in[:, :, :qk_w]), w_in[:, :, qk_w:]],
                             axis=-1).astype(BF16)
    gq = _permute_head_lanes(q_norm) * SOFTMAX_SCALE_LOG2
    gk = _permute_head_lanes(k_norm)
    qk_gains = jnp.concatenate(
        [jnp.broadcast_to(gq[:, None, None, :], (depth, d_model // PROJ_TN, 1, HEAD_DIM)),
         jnp.broadcast_to(gk[:, None, None, :], (depth, kv_w // PROJ_TN, 1, HEAD_DIM))], axis=1)
    row = lambda v: v.reshape(depth, 1, v.shape[-1])
    w = {
        "w_in": w_in_b, "qk_gains": qk_gains,
        "w_pool": w_pool.astype(BF16), "pool_scale": row(pool_scale),
        "w_o": w_o.astype(BF16),
        "ln1_g": row(ln1_g), "ln1_b": row(ln1_b),
        "w_up": w_up.astype(BF16), "w_down": w_down.astype(BF16),
        "ln2_g": row(ln2_g), "ln2_b": row(ln2_b),
    }
    return (_trunk(x_prompt, w), _trunk(x_sample, w))
```

```python
import functools

import jax
import jax.numpy as jnp
from jax import lax
from jax.experimental import pallas as pl
from jax.experimental.pallas import tpu as pltpu

F32 = jnp.float32
BF16 = jnp.bfloat16

HEAD_DIM = 128
N_KV_HEADS = 4
GRID_W = 64
ROPE_THETA = 10000.0
POOL_WINDOWS = (2, 4, 8, 16)
POOL_HALO = 8
assert all(w >= 2 and w & (w - 1) == 0 and w // 2 <= POOL_HALO for w in POOL_WINDOWS)
LN_EPS = 1e-5
QK_EPS = 1e-6
LOG2E = 1.4426950408889634
SOFTMAX_SCALE_LOG2 = HEAD_DIM ** -0.5 * LOG2E
NEG_BIG = -1e30
ROPE_ROW_BLOCK = 128
PROJ_TN = 512

VMEM_LIMIT_BYTES = 56 * 1024 * 1024


def _tile(n, pref):
    t = min(n, pref)
    assert n % t == 0, (n, pref)
    return t


def _params(sem):
    return pltpu.CompilerParams(dimension_semantics=sem,
                                vmem_limit_bytes=VMEM_LIMIT_BYTES)


def _permute_head_lanes(a):
    lead = a.shape[:-1]
    n = a.shape[-1] // HEAD_DIM
    a = a.reshape(lead + (n, 2, 2, HEAD_DIM // 4))
    a = jnp.swapaxes(a, -3, -2)
    return a.reshape(lead + (n * HEAD_DIM,))


def _rope_tables(seq):
    t = jnp.arange(seq)
    row = (t // GRID_W).astype(F32)
    col = (t % GRID_W).astype(F32)
    axis_dim = HEAD_DIM // 2
    inv_freq = ROPE_THETA ** (-jnp.arange(0, axis_dim, 2, dtype=F32) / axis_dim)
    ar = row[:, None] * inv_freq[None, :]
    ac = col[:, None] * inv_freq[None, :]
    cos = jnp.concatenate([jnp.cos(ar), jnp.cos(ac), jnp.cos(ar), jnp.cos(ac)], axis=-1)
    sin = jnp.concatenate([-jnp.sin(ar), -jnp.sin(ac), jnp.sin(ar), jnp.sin(ac)], axis=-1)
    return cos, sin


def _proj_kernel(x_ref, wqk_ref, w_ref, g_ref, cos_ref, sin_ref, qk_ref, vT_ref, u_ref, gt_ref,
                 acc_ref, *maybe_xb_ref, rb, j_v, j_u, j_gate):
    tm, tn = acc_ref.shape
    j = pl.program_id(2)
    heads = [slice(h * HEAD_DIM, (h + 1) * HEAD_DIM) for h in range(tn // HEAD_DIM)]
    row_blocks = [slice(r * rb, (r + 1) * rb) for r in range(tm // rb)]

    lhs_ref = x_ref
    if maybe_xb_ref:
        (lhs_ref,) = maybe_xb_ref

        @pl.when(j == 0)
        def _():
            lhs_ref[...] = x_ref[...].astype(lhs_ref.dtype)

    def interleaved(weights_ref, epilogue):
        def block_matmul(rows):
            acc_ref[rows, :] = jnp.dot(lhs_ref[rows, :], weights_ref[...],
                                       preferred_element_type=F32)

        block_matmul(row_blocks[0])
        for r, rows in enumerate(row_blocks):
            if r + 1 < len(row_blocks):
                block_matmul(row_blocks[r + 1])
            epilogue(rows)

    def rope_epilogue(rows):
        gain = g_ref[...]
        cos = cos_ref[rows, :]
        sin = sin_ref[rows, :]
        for cols in heads:
            xh = acc_ref[rows, cols]
            ms = jnp.mean(xh * xh, axis=-1, keepdims=True)
            y = xh * lax.rsqrt(ms + QK_EPS) * gain
            rot = y * cos + pltpu.roll(y, HEAD_DIM // 2, 1) * sin
            qk_ref[rows, cols] = rot.astype(qk_ref.dtype)

    def v_epilogue(rows):
        for h, cols in enumerate(heads):
            vT_ref[h, :, rows] = acc_ref[rows, cols].T.astype(vT_ref.dtype)

    def gate_epilogue(rows):
        gt_ref[rows, :] = 1.0 / (1.0 + jnp.exp(-acc_ref[rows, :]))

    pl.when(j < j_v)(lambda: interleaved(wqk_ref, rope_epilogue))
    pl.when(j == j_v)(lambda: interleaved(w_ref, v_epilogue))
    pl.when(j >= j_gate)(lambda: interleaved(w_ref, gate_epilogue))

    @pl.when((j >= j_u) & (j < j_gate))
    def _():
        u_ref[...] = jnp.dot(lhs_ref[...], w_ref[...], preferred_element_type=F32)


def _proj(x, w_qk, w, gains, cos, sin, *, layer, v_w, u_w, tm_pref=1024, tn=512):
    B, S, D = x.shape
    N = w.shape[2]
    qk_w = w_qk.shape[2]
    assert v_w == tn and qk_w % tn == 0 and u_w % tn == 0 and N % tn == 0
    tm = _tile(S, tm_pref)
    n_i = S // tm
    rb = _tile(tm, ROPE_ROW_BLOCK)
    j_v = qk_w // tn
    j_u = j_v + 1
    j_gate = j_u + u_w // tn
    gate_w = N - j_gate * tn
    kernel = functools.partial(_proj_kernel, rb=rb, j_v=j_v, j_u=j_u, j_gate=j_gate)
    clamp = lambda v, lo, hi: jnp.minimum(jnp.maximum(v, lo), hi)
    if x.dtype == BF16:
        x_map = lambda b, i, j: (b, i, 0)
    else:
        def x_map(b, i, j):
            r = jnp.minimum(b * n_i + i + (j > 0).astype(jnp.int32), B * n_i - 1)
            return r // n_i, r % n_i, 0
    return pl.pallas_call(
        kernel,
        grid=(B, n_i, N // tn),
        in_specs=[
            pl.BlockSpec((None, tm, D), x_map),
            pl.BlockSpec((None, D, tn), lambda b, i, j: (layer, 0, jnp.minimum(j, j_v - 1))),
            pl.BlockSpec((None, D, tn), lambda b, i, j: (layer, 0, jnp.maximum(j, j_v))),
            pl.BlockSpec((None, None, 1, HEAD_DIM),
                         lambda b, i, j: (layer, jnp.minimum(j, j_v - 1), 0, 0)),
            pl.BlockSpec((tm, HEAD_DIM), lambda b, i, j: (i, 0)),
            pl.BlockSpec((tm, HEAD_DIM), lambda b, i, j: (i, 0)),
        ],
        out_specs=[
            pl.BlockSpec((None, tm, tn), lambda b, i, j: (b, i, jnp.minimum(j, j_v - 1))),
            pl.BlockSpec((None, tn // HEAD_DIM, HEAD_DIM, tm), lambda b, i, j: (b, 0, 0, i)),
            pl.BlockSpec((None, tm, tn), lambda b, i, j: (b, i, clamp(j - j_u, 0, j_gate - j_u - 1))),
            pl.BlockSpec((None, tm, tn), lambda b, i, j: (b, i, jnp.maximum(j - j_gate, 0))),
        ],
        out_shape=[
            jax.ShapeDtypeStruct((B, S, qk_w), BF16),
            jax.ShapeDtypeStruct((B, v_w // HEAD_DIM, HEAD_DIM, S), BF16),
            jax.ShapeDtypeStruct((B, S, u_w), F32),
            jax.ShapeDtypeStruct((B, S, gate_w), F32),
        ],
        scratch_shapes=[pltpu.VMEM((tm, tn), F32)]
        + ([] if x.dtype == BF16 else [pltpu.VMEM((tm, D), BF16)]),
        compiler_params=_params(("parallel", "parallel", "arbitrary")),
        name="proj",
    )(x, w_qk, w, gains, cos, sin)


def _flash_kernel(q_ref, qn_ref, k_ref, kn_ref, vT_ref, o_ref,
                  qT_ref, qTn_ref, m_ref, l_ref, acc_ref, s_ref, p_ref, mx_ref,
                  *, tk, ts, n_chunks, gh):
    G = qT_ref.shape[0]
    tq = qT_ref.shape[2]
    nt = tk // ts

    def slot_window(g):
        sb, h = divmod(g, gh)
        return slice(sb * tq, (sb + 1) * tq), slice(h * HEAD_DIM, (h + 1) * HEAD_DIM)

    def transpose_heads(src_ref, dst_ref):
        for g in range(G):
            rows, cols = slot_window(g)
            dst_ref[g] = src_ref[rows, cols].astype(F32).T.astype(BF16)

    def sub8(x):
        return x.reshape(ts // 8, 8, tq)

    def qk_tile(qt_ref, keys_ref, off, g, t):
        kt = keys_ref[pl.ds(off + t * ts, ts), :]
        s = jnp.dot(kt, qt_ref[g], preferred_element_type=F32)
        s_ref[g, t * ts:(t + 1) * ts, :] = s
        return jnp.max(sub8(s), axis=0)

    def exp_tile(g, t, m_new):
        p = jnp.exp2(s_ref[g, t * ts:(t + 1) * ts, :] - m_new)
        p_ref[g, t * ts:(t + 1) * ts, :] = p.astype(BF16)
        return jnp.sum(sub8(p), axis=0)

    def chunk_step(off, qt_next_ref, keys_next_ref, off_next):
        for g in range(G):
            m_old = m_ref[g]
            m_new = jnp.maximum(m_old, jnp.max(mx_ref[g], axis=0, keepdims=True))
            alpha = jnp.exp2(m_old - m_new)
            m_ref[g] = m_new
            lsum = None
            mx = None
            for t in range(nt):
                ls = exp_tile(g, t, m_new)
                lsum = ls if lsum is None else lsum + ls
                tile_mx = qk_tile(qt_next_ref, keys_next_ref, off_next, g, t)
                mx = tile_mx if mx is None else jnp.maximum(mx, tile_mx)
            l_ref[g] = alpha * l_ref[g] + lsum
            acc_ref[g] = alpha * acc_ref[g] + jnp.dot(
                vT_ref[:, pl.ds(off, tk)], p_ref[g], preferred_element_type=F32)
            mx_ref[g] = mx

    m_ref[...] = jnp.full(m_ref.shape, NEG_BIG, F32)
    l_ref[...] = jnp.zeros(l_ref.shape, F32)
    acc_ref[...] = jnp.zeros(acc_ref.shape, F32)

    @pl.when((pl.program_id(0) == 0) & (pl.program_id(1) == 0) & (pl.program_id(2) == 0))
    def _():
        transpose_heads(q_ref, qT_ref)
        for g in range(G):
            mx = None
            for t in range(nt):
                tile_mx = qk_tile(qT_ref, k_ref, 0, g, t)
                mx = tile_mx if mx is None else jnp.maximum(mx, tile_mx)
            mx_ref[g] = mx

    transpose_heads(qn_ref, qTn_ref)

    def body(c, carry):
        off = pl.multiple_of(c * tk, tk)
        chunk_step(off, qT_ref, k_ref, pl.multiple_of(off + tk, tk))
        return carry

    lax.fori_loop(0, n_chunks - 1, body, 0)
    chunk_step((n_chunks - 1) * tk, qTn_ref, kn_ref, 0)

    for g in range(G):
        o = acc_ref[g] / jnp.sum(l_ref[g], axis=0, keepdims=True)
        rows, cols = slot_window(g)
        o_ref[rows, cols] = o.T.astype(o_ref.dtype)
    qT_ref[...] = qTn_ref[...]


def _flash(qk, vT, *, n_heads, tq_pref=256, nsub_pref=4, tk_pref=1024, ts_pref=128):
    B, KV, hd, S = vT.shape
    gh = n_heads // KV
    tq = _tile(S, tq_pref)
    nsub = _tile(S // tq, nsub_pref)
    tqb = nsub * tq
    G = nsub * gh
    tk = _tile(S, tk_pref)
    ts = _tile(tk, ts_pref)
    nq = S // tqb
    kernel = functools.partial(_flash_kernel, tk=tk, ts=ts, n_chunks=S // tk, gh=gh)

    def next_step(b, h, i):
        wrap = i == nq - 1
        flat = jnp.minimum(b * KV + h + wrap.astype(jnp.int32), B * KV - 1)
        last = (b * KV + h == B * KV - 1) & wrap
        return flat // KV, flat % KV, jnp.where(wrap & ~last, 0, jnp.minimum(i + 1, nq - 1))

    def next_q(b, h, i):
        nb, nh, ni = next_step(b, h, i)
        return nb, ni, nh

    def next_k(b, h, i):
        nb, nh, _ = next_step(b, h, i)
        return nb, 0, n_heads + nh

    return pl.pallas_call(
        kernel,
        grid=(B, KV, nq),
        in_specs=[
            pl.BlockSpec((None, tqb, gh * hd), lambda b, h, i: (b, i, h)),
            pl.BlockSpec((None, tqb, gh * hd), next_q),
            pl.BlockSpec((None, S, hd), lambda b, h, i: (b, 0, n_heads + h)),
            pl.BlockSpec((None, tk, hd), next_k),
            pl.BlockSpec((None, None, hd, S), lambda b, h, i: (b, h, 0, 0)),
        ],
        out_specs=pl.BlockSpec((None, tqb, gh * hd), lambda b, h, i: (b, i, h)),
        out_shape=jax.ShapeDtypeStruct((B, S, n_heads * hd), F32),
        scratch_shapes=[
            pltpu.VMEM((G, hd, tq), BF16),
            pltpu.VMEM((G, hd, tq), BF16),
            pltpu.VMEM((G, 1, tq), F32),
            pltpu.VMEM((G, 8, tq), F32),
            pltpu.VMEM((G, hd, tq), F32),
            pltpu.VMEM((G, tk, tq), F32),
            pltpu.VMEM((G, tk, tq), BF16),
            pltpu.VMEM((G, 8, tq), F32),
        ],
        compiler_params=_params(("arbitrary", "arbitrary", "arbitrary")),
        name="flash_attn",
    )(qk, qk, qk, qk, vT)


def _layer_norm(z, g, b):
    mu = jnp.mean(z, axis=-1, keepdims=True)
    zc = z - mu
    var = jnp.mean(zc * zc, axis=-1, keepdims=True)
    return zc * lax.rsqrt(var + LN_EPS) * g + b


def _mix_kernel(a_ref, gt_ref, u_ref, up_ref, un_ref, x_ref, wp_ref, ps_ref, wo_ref,
                g_ref, b_ref, o32_ref, o16_ref, pad_ref, *, alpha, seq):
    tm = u_ref.shape[0]
    d_model = a_ref.shape[1]
    n_groups = len(POOL_WINDOWS)
    gin = u_ref.shape[1] // n_groups
    i = pl.program_id(1)
    nblk = pl.num_programs(1)
    pad_ref[0:POOL_HALO, :] = jnp.where(i > 0, up_ref[...], 0.0)
    pad_ref[POOL_HALO:POOL_HALO + tm, :] = u_ref[...]
    pad_ref[POOL_HALO + tm:, :] = jnp.where(i < nblk - 1, un_ref[...], 0.0)

    t = i * tm + lax.broadcasted_iota(jnp.int32, (tm, 1), 0)
    n_pad = tm + 2 * POOL_HALO
    outs = []
    for g, w in enumerate(POOL_WINDOWS):
        lo_off = -(w // 2)
        hi_off = w - 1 - w // 2
        c0 = g * gin
        win = pad_ref[:, c0:c0 + gin]
        win = win + pltpu.roll(win, 1, 0)
        width = 2
        while width < w:
            half = width // 2
            win = pltpu.roll(win, n_pad - half, 0) + pltpu.roll(win, half, 0)
            width *= 2
        tot = win[POOL_HALO:POOL_HALO + tm]
        lo = jnp.maximum(t + lo_off, 0)
        hi = jnp.minimum(t + hi_off, seq - 1)
        cnt = (hi - lo + 1).astype(F32)
        pooled = tot / cnt - u_ref[:, c0:c0 + gin]
        outs.append(jnp.dot(pooled.astype(BF16), wp_ref[g], preferred_element_type=F32))
    p = jnp.concatenate(outs, axis=-1) * ps_ref[...]
    m = gt_ref[:, :d_model] * a_ref[...] + gt_ref[:, d_model:] * p
    y = jnp.dot(m.astype(BF16), wo_ref[...], preferred_element_type=F32)
    out = _layer_norm(alpha * x_ref[...] + y, g_ref[...], b_ref[...])
    o32_ref[...] = out
    o16_ref[...] = out.astype(BF16)


def _mix(a, gates, u, x, w_pool, pool_scale, w_o, ln_g, ln_b, *, layer, alpha, tm_pref=256):
    B, S, D = x.shape
    U = u.shape[-1]
    tm = _tile(S, tm_pref)
    hb = tm // POOL_HALO
    n_halo_blocks = S // POOL_HALO
    kernel = functools.partial(_mix_kernel, alpha=alpha, seq=S)
    row = lambda b, i: (b, i, 0)
    vec = pl.BlockSpec((None, 1, D), lambda b, i: (layer, 0, 0))
    return pl.pallas_call(
        kernel,
        grid=(B, S // tm),
        in_specs=[
            pl.BlockSpec((None, tm, D), row),
            pl.BlockSpec((None, tm, 2 * D), row),
            pl.BlockSpec((None, tm, U), row),
            pl.BlockSpec((None, POOL_HALO, U),
                         lambda b, i: (b, jnp.maximum(i * hb - 1, 0), 0)),
            pl.BlockSpec((None, POOL_HALO, U),
                         lambda b, i: (b, jnp.minimum((i + 1) * hb, n_halo_blocks - 1), 0)),
            pl.BlockSpec((None, tm, D), row),
            pl.BlockSpec((None,) + w_pool.shape[1:], lambda b, i: (layer, 0, 0, 0)),
            vec,
            pl.BlockSpec((None, D, D), lambda b, i: (layer, 0, 0)),
            vec,
            vec,
        ],
        out_specs=[pl.BlockSpec((None, tm, D), row), pl.BlockSpec((None, tm, D), row)],
        out_shape=[jax.ShapeDtypeStruct((B, S, D), F32), jax.ShapeDtypeStruct((B, S, D), BF16)],
        scratch_shapes=[pltpu.VMEM((tm + 2 * POOL_HALO, U), F32)],
        compiler_params=_params(("parallel", "arbitrary")),
        name="mix_wo_ln",
    )(a, gates, u, u, u, x, w_pool, pool_scale, w_o, ln_g, ln_b)


def _mlp_kernel(xb_ref, x_ref, wu_ref, wd_ref, g_ref, b_ref, o32_ref, o16_ref, acc_ref, *, alpha):
    r = pl.program_id(0)
    f = pl.program_id(1)
    n_rows = pl.num_programs(0) - 1

    def down_of_up():
        h = jnp.dot(xb_ref[...], wu_ref[...], preferred_element_type=F32)
        h = jnp.square(jnp.maximum(h, 0.0))
        return jnp.dot(h.astype(BF16), wd_ref[...], preferred_element_type=F32)

    def norm_previous():
        out = _layer_norm(alpha * x_ref[...] + acc_ref[...], g_ref[...], b_ref[...])
        o32_ref[...] = out
        o16_ref[...] = out.astype(BF16)

    @pl.when((f == 0) & (r == 0))
    def _():
        acc_ref[...] = down_of_up()

    @pl.when((f == 0) & (r > 0) & (r < n_rows))
    def _():
        norm_previous()
        acc_ref[...] = down_of_up()

    @pl.when((f == 0) & (r == n_rows))
    def _():
        norm_previous()

    @pl.when((f > 0) & (r < n_rows))
    def _():
        acc_ref[...] += down_of_up()


def _mlp(xb, x, w_up, w_down, ln_g, ln_b, *, layer, alpha, tm_pref=512, tf_pref=1024):
    B, S, D = x.shape
    F = w_up.shape[2]
    tm = _tile(S, tm_pref)
    tf = _tile(F, tf_pref)
    n_rows = B * S // tm
    kernel = functools.partial(_mlp_kernel, alpha=alpha)
    summed = lambda r, f: (jnp.minimum(r, n_rows - 1), 0)
    normed = lambda r, f: (jnp.maximum(r - 1, 0), 0)
    vec = pl.BlockSpec((None, 1, D), lambda r, f: (layer, 0, 0))
    out32, out16 = pl.pallas_call(
        kernel,
        grid=(n_rows + 1, F // tf),
        in_specs=[
            pl.BlockSpec((tm, D), summed),
            pl.BlockSpec((tm, D), normed),
            pl.BlockSpec((None, D, tf), lambda r, f: (layer, 0, f)),
            pl.BlockSpec((None, tf, D), lambda r, f: (layer, f, 0)),
            vec,
            vec,
        ],
        out_specs=[pl.BlockSpec((tm, D), normed), pl.BlockSpec((tm, D), normed)],
        out_shape=[jax.ShapeDtypeStruct((B * S, D), F32), jax.ShapeDtypeStruct((B * S, D), BF16)],
        scratch_shapes=[pltpu.VMEM((tm, D), F32)],
        compiler_params=_params(("arbitrary", "arbitrary")),
        name="mlp_ln",
    )(xb.reshape(B * S, D), x.reshape(B * S, D), w_up, w_down, ln_g, ln_b)
    return out32.reshape(B, S, D), out16.reshape(B, S, D)


def _trunk(x, w):
    B, S, D = x.shape
    depth = w["w_in"].shape[0]
    alpha = (2 * depth) ** 0.25
    n_heads = D // HEAD_DIM
    kv_w = N_KV_HEADS * HEAD_DIM
    cos, sin = _rope_tables(S)
    xb = x
    for l in range(depth):
        qk, vT, u, gates = _proj(xb, w["w_qk"], w["w_in"], w["qk_gains"], cos, sin, layer=l,
                                 v_w=kv_w, u_w=D // 2, tn=PROJ_TN)
        a = _flash(qk, vT, n_heads=n_heads)
        x, xb = _mix(a, gates, u, x, w["w_pool"], w["pool_scale"], w["w_o"],
                     w["ln1_g"], w["ln1_b"], layer=l, alpha=alpha)
        x, xb = _mlp(xb, x, w["w_up"], w["w_down"], w["ln2_g"], w["ln2_b"], layer=l, alpha=alpha)
    return x


def kernel(x_prompt, x_sample, w_in, q_norm, k_norm, w_pool, pool_scale, w_o,
           ln1_g, ln1_b, w_up, w_down, ln2_g, ln2_b):
    depth, d_model, _ = w_in.shape
    kv_w = N_KV_HEADS * HEAD_DIM
    qk_w = d_model + kv_w
    w_in_b = w_in.astype(BF16)
    gq = _permute_head_lanes(q_norm) * SOFTMAX_SCALE_LOG2
    gk = _permute_head_lanes(k_norm)
    qk_gains = jnp.concatenate(
        [jnp.broadcast_to(gq[:, None, None, :], (depth, d_model // PROJ_TN, 1, HEAD_DIM)),
         jnp.broadcast_to(gk[:, None, None, :], (depth, kv_w // PROJ_TN, 1, HEAD_DIM))], axis=1)
    row = lambda v: v.reshape(depth, 1, v.shape[-1])
    w = {
        "w_in": w_in_b, "w_qk": _permute_head_lanes(w_in_b[:, :, :qk_w]), "qk_gains": qk_gains,
        "w_pool": w_pool.astype(BF16), "pool_scale": row(pool_scale),
        "w_o": w_o.astype(BF16),
        "ln1_g": row(ln1_g), "ln1_b": row(ln1_b),
        "w_up": w_up.astype(BF16), "w_down": w_down.astype(BF16),
        "ln2_g": row(ln2_g), "ln2_b": row(ln2_b),
    }
    return (_trunk(x_prompt, w), _trunk(x_sample, w))
```

```python
import functools

import jax
import jax.numpy as jnp
from jax import lax
from jax.experimental import pallas as pl
from jax.experimental.pallas import tpu as pltpu

F32 = jnp.float32
BF16 = jnp.bfloat16

HEAD_DIM = 128
N_KV_HEADS = 4
GRID_W = 64
ROPE_THETA = 10000.0
POOL_WINDOWS = (2, 4, 8, 16)
POOL_HALO = 8
assert all(w >= 2 and w & (w - 1) == 0 and w // 2 <= POOL_HALO for w in POOL_WINDOWS)
LN_EPS = 1e-5
QK_EPS = 1e-6
LOG2E = 1.4426950408889634
SOFTMAX_SCALE_LOG2 = HEAD_DIM ** -0.5 * LOG2E
NEG_BIG = -1e30
ROPE_ROW_BLOCK = 128
PROJ_TN = 512

VMEM_LIMIT_BYTES = 56 * 1024 * 1024


def _tile(n, pref):
    t = min(n, pref)
    assert n % t == 0, (n, pref)
    return t


def _params(sem):
    return pltpu.CompilerParams(dimension_semantics=sem,
                                vmem_limit_bytes=VMEM_LIMIT_BYTES)


def _permute_head_lanes(a):
    lead = a.shape[:-1]
    n = a.shape[-1] // HEAD_DIM
    a = a.reshape(lead + (n, 2, 2, HEAD_DIM // 4))
    a = jnp.swapaxes(a, -3, -2)
    return a.reshape(lead + (n * HEAD_DIM,))


def _rope_tables(seq):
    t = jnp.arange(seq)
    row = (t // GRID_W).astype(F32)
    col = (t % GRID_W).astype(F32)
    axis_dim = HEAD_DIM // 2
    inv_freq = ROPE_THETA ** (-jnp.arange(0, axis_dim, 2, dtype=F32) / axis_dim)
    ar = row[:, None] * inv_freq[None, :]
    ac = col[:, None] * inv_freq[None, :]
    cos = jnp.concatenate([jnp.cos(ar), jnp.cos(ac), jnp.cos(ar), jnp.cos(ac)], axis=-1)
    sin = jnp.concatenate([-jnp.sin(ar), -jnp.sin(ac), jnp.sin(ar), jnp.sin(ac)], axis=-1)
    return cos, sin


def _proj_kernel(x_ref, wqk_ref, w_ref, g_ref, cos_ref, sin_ref, qk_ref, vT_ref, u_ref, gt_ref,
                 acc_ref, *maybe_xb_ref, rb, j_v, j_u, j_gate):
    tm, tn = acc_ref.shape
    j = pl.program_id(2)
    heads = [slice(h * HEAD_DIM, (h + 1) * HEAD_DIM) for h in range(tn // HEAD_DIM)]
    row_blocks = [slice(r * rb, (r + 1) * rb) for r in range(tm // rb)]

    lhs_ref = x_ref
    if maybe_xb_ref:
        (lhs_ref,) = maybe_xb_ref

        @pl.when(j == 0)
        def _():
            lhs_ref[...] = x_ref[...].astype(lhs_ref.dtype)

    def interleaved(weights_ref, epilogue):
        def block_matmul(rows):
            acc_ref[rows, :] = jnp.dot(lhs_ref[rows, :], weights_ref[...],
                                       preferred_element_type=F32)

        block_matmul(row_blocks[0])
        for r, rows in enumerate(row_blocks):
            if r + 1 < len(row_blocks):
                block_matmul(row_blocks[r + 1])
            epilogue(rows)

    def rope_epilogue(rows):
        gain = g_ref[...]
        cos = cos_ref[rows, :]
        sin = sin_ref[rows, :]
        for cols in heads:
            xh = acc_ref[rows, cols]
            ms = jnp.mean(xh * xh, axis=-1, keepdims=True)
            y = xh * lax.rsqrt(ms + QK_EPS) * gain
            rot = y * cos + pltpu.roll(y, HEAD_DIM // 2, 1) * sin
            qk_ref[rows, cols] = rot.astype(qk_ref.dtype)

    def v_epilogue(rows):
        for h, cols in enumerate(heads):
            vT_ref[h, :, rows] = acc_ref[rows, cols].T.astype(vT_ref.dtype)

    def gate_epilogue(rows):
        gt_ref[rows, :] = 1.0 / (1.0 + jnp.exp(-acc_ref[rows, :]))

    pl.when(j < j_v)(lambda: interleaved(wqk_ref, rope_epilogue))
    pl.when(j == j_v)(lambda: interleaved(w_ref, v_epilogue))
    pl.when(j >= j_gate)(lambda: interleaved(w_ref, gate_epilogue))

    @pl.when((j >= j_u) & (j < j_gate))
    def _():
        u_ref[...] = jnp.dot(lhs_ref[...], w_ref[...], preferred_element_type=F32)


def _proj(x, w_qk, w, gains, cos, sin, *, layer, v_w, u_w, tm_pref=1024, tn=512):
    B, S, D = x.shape
    N = w.shape[2]
    qk_w = w_qk.shape[2]
    assert v_w == tn and qk_w % tn == 0 and u_w % tn == 0 and N % tn == 0
    tm = _tile(S, tm_pref)
    n_i = S // tm
    rb = _tile(tm, ROPE_ROW_BLOCK)
    j_v = qk_w // tn
    j_u = j_v + 1
    j_gate = j_u + u_w // tn
    gate_w = N - j_gate * tn
    kernel = functools.partial(_proj_kernel, rb=rb, j_v=j_v, j_u=j_u, j_gate=j_gate)
    n_j = N // tn
    n_rows = B * n_i

    def following(b, i):
        r = jnp.minimum(b * n_i + i + 1, n_rows - 1)
        return r // n_i, r % n_i

    def pick(done, b, i, b_next, i_next):
        move = done & (b * n_i + i < n_rows - 1)
        return jnp.where(move, b_next, b), jnp.where(move, i_next, i), move

    if x.dtype == BF16:
        x_map = lambda b, i, j: (b, i, 0)
    else:
        def x_map(b, i, j):
            b_next, i_next = following(b, i)
            return jnp.where(j > 0, b_next, b), jnp.where(j > 0, i_next, i), 0

    def qk_map(b, i, j):
        bb, ii, move = pick(j >= j_v, b, i, *following(b, i))
        return bb, ii, jnp.where(move, 0, jnp.minimum(j, j_v - 1))

    def vT_map(b, i, j):
        bb, ii, _ = pick(j > j_v, b, i, *following(b, i))
        return bb, 0, 0, ii

    def u_map(b, i, j):
        bb, ii, move = pick(j >= j_gate, b, i, *following(b, i))
        return bb, ii, jnp.where(move, 0, jnp.minimum(jnp.maximum(j - j_u, 0), j_gate - j_u - 1))

    return pl.pallas_call(
        kernel,
        grid=(B, n_i, n_j),
        in_specs=[
            pl.BlockSpec((None, tm, D), x_map),
            pl.BlockSpec((None, D, tn), lambda b, i, j: (layer, 0, jnp.minimum(j, j_v - 1))),
            pl.BlockSpec((None, D, tn), lambda b, i, j: (layer, 0, jnp.where(j < j_v, n_j - 1, j))),
            pl.BlockSpec((None, None, 1, HEAD_DIM),
                         lambda b, i, j: (layer, jnp.minimum(j, j_v - 1), 0, 0)),
            pl.BlockSpec((tm, HEAD_DIM), lambda b, i, j: (i, 0)),
            pl.BlockSpec((tm, HEAD_DIM), lambda b, i, j: (i, 0)),
        ],
        out_specs=[
            pl.BlockSpec((None, tm, tn), qk_map),
            pl.BlockSpec((None, tn // HEAD_DIM, HEAD_DIM, tm), vT_map),
            pl.BlockSpec((None, tm, tn), u_map),
            pl.BlockSpec((None, tm, tn), lambda b, i, j: (b, i, jnp.maximum(j - j_gate, 0))),
        ],
        out_shape=[
            jax.ShapeDtypeStruct((B, S, qk_w), BF16),
            jax.ShapeDtypeStruct((B, v_w // HEAD_DIM, HEAD_DIM, S), BF16),
            jax.ShapeDtypeStruct((B, S, u_w), F32),
            jax.ShapeDtypeStruct((B, S, gate_w), F32),
        ],
        scratch_shapes=[pltpu.VMEM((tm, tn), F32)]
        + ([] if x.dtype == BF16 else [pltpu.VMEM((tm, D), BF16)]),
        compiler_params=_params(("arbitrary", "arbitrary", "arbitrary")),
        name="proj",
    )(x, w_qk, w, gains, cos, sin)


def _flash_kernel(q_ref, qn_ref, k_ref, kn_ref, vT_ref, o_ref,
                  qT_ref, qTn_ref, m_ref, l_ref, acc_ref, s_ref, p_ref, mx_ref,
                  *, tk, ts, n_chunks, gh):
    G = qT_ref.shape[0]
    tq = qT_ref.shape[2]
    nt = tk // ts

    def slot_window(g):
        sb, h = divmod(g, gh)
        return slice(sb * tq, (sb + 1) * tq), slice(h * HEAD_DIM, (h + 1) * HEAD_DIM)

    def transpose_heads(src_ref, dst_ref):
        for g in range(G):
            rows, cols = slot_window(g)
            dst_ref[g] = src_ref[rows, cols].astype(F32).T.astype(BF16)

    def sub8(x):
        return x.reshape(ts // 8, 8, tq)

    def qk_tile(qt_ref, keys_ref, off, g, t):
        kt = keys_ref[pl.ds(off + t * ts, ts), :]
        s = jnp.dot(kt, qt_ref[g], preferred_element_type=F32)
        s_ref[g, t * ts:(t + 1) * ts, :] = s
        return jnp.max(sub8(s), axis=0)

    def exp_tile(g, t, m_new):
        p = jnp.exp2(s_ref[g, t * ts:(t + 1) * ts, :] - m_new)
        p_ref[g, t * ts:(t + 1) * ts, :] = p.astype(BF16)
        return jnp.sum(sub8(p), axis=0)

    def chunk_step(off, qt_next_ref, keys_next_ref, off_next):
        for g in range(G):
            m_old = m_ref[g]
            m_new = jnp.maximum(m_old, jnp.max(mx_ref[g], axis=0, keepdims=True))
            alpha = jnp.exp2(m_old - m_new)
            m_ref[g] = m_new
            lsum = None
            mx = None
            for t in range(nt):
                ls = exp_tile(g, t, m_new)
                lsum = ls if lsum is None else lsum + ls
                tile_mx = qk_tile(qt_next_ref, keys_next_ref, off_next, g, t)
                mx = tile_mx if mx is None else jnp.maximum(mx, tile_mx)
            l_ref[g] = alpha * l_ref[g] + lsum
            acc_ref[g] = alpha * acc_ref[g] + jnp.dot(
                vT_ref[:, pl.ds(off, tk)], p_ref[g], preferred_element_type=F32)
            mx_ref[g] = mx

    m_ref[...] = jnp.full(m_ref.shape, NEG_BIG, F32)
    l_ref[...] = jnp.zeros(l_ref.shape, F32)
    acc_ref[...] = jnp.zeros(acc_ref.shape, F32)

    @pl.when((pl.program_id(0) == 0) & (pl.program_id(1) == 0) & (pl.program_id(2) == 0))
    def _():
        transpose_heads(q_ref, qT_ref)
        for g in range(G):
            mx = None
            for t in range(nt):
                tile_mx = qk_tile(qT_ref, k_ref, 0, g, t)
                mx = tile_mx if mx is None else jnp.maximum(mx, tile_mx)
            mx_ref[g] = mx

    transpose_heads(qn_ref, qTn_ref)

    def body(c, carry):
        off = pl.multiple_of(c * tk, tk)
        chunk_step(off, qT_ref, k_ref, pl.multiple_of(off + tk, tk))
        return carry

    lax.fori_loop(0, n_chunks - 1, body, 0)
    chunk_step((n_chunks - 1) * tk, qTn_ref, kn_ref, 0)

    for g in range(G):
        o = acc_ref[g] / jnp.sum(l_ref[g], axis=0, keepdims=True)
        rows, cols = slot_window(g)
        o_ref[rows, cols] = o.T.astype(o_ref.dtype)
    qT_ref[...] = qTn_ref[...]


def _flash(qk, vT, *, n_heads, tq_pref=256, nsub_pref=4, tk_pref=1024, ts_pref=128):
    B, KV, hd, S = vT.shape
    gh = n_heads // KV
    tq = _tile(S, tq_pref)
    nsub = _tile(S // tq, nsub_pref)
    tqb = nsub * tq
    G = nsub * gh
    tk = _tile(S, tk_pref)
    ts = _tile(tk, ts_pref)
    nq = S // tqb
    kernel = functools.partial(_flash_kernel, tk=tk, ts=ts, n_chunks=S // tk, gh=gh)

    def next_step(b, h, i):
        wrap = i == nq - 1
        flat = jnp.minimum(b * KV + h + wrap.astype(jnp.int32), B * KV - 1)
        last = (b * KV + h == B * KV - 1) & wrap
        return flat // KV, flat % KV, jnp.where(wrap & ~last, 0, jnp.minimum(i + 1, nq - 1))

    def next_q(b, h, i):
        nb, nh, ni = next_step(b, h, i)
        return nb, ni, nh

    def next_k(b, h, i):
        nb, nh, _ = next_step(b, h, i)
        return nb, 0, n_heads + nh

    return pl.pallas_call(
        kernel,
        grid=(B, KV, nq),
        in_specs=[
            pl.BlockSpec((None, tqb, gh * hd), lambda b, h, i: (b, i, h)),
            pl.BlockSpec((None, tqb, gh * hd), next_q),
            pl.BlockSpec((None, S, hd), lambda b, h, i: (b, 0, n_heads + h)),
            pl.BlockSpec((None, tk, hd), next_k),
            pl.BlockSpec((None, None, hd, S), lambda b, h, i: (b, h, 0, 0)),
        ],
        out_specs=pl.BlockSpec((None, tqb, gh * hd), lambda b, h, i: (b, i, h)),
        out_shape=jax.ShapeDtypeStruct((B, S, n_heads * hd), F32),
        scratch_shapes=[
            pltpu.VMEM((G, hd, tq), BF16),
            pltpu.VMEM((G, hd, tq), BF16),
            pltpu.VMEM((G, 1, tq), F32),
            pltpu.VMEM((G, 8, tq), F32),
            pltpu.VMEM((G, hd, tq), F32),
            pltpu.VMEM((G, tk, tq), F32),
            pltpu.VMEM((G, tk, tq), BF16),
            pltpu.VMEM((G, 8, tq), F32),
        ],
        compiler_params=_params(("arbitrary", "arbitrary", "arbitrary")),
        name="flash_attn",
    )(qk, qk, qk, qk, vT)


def _layer_norm(z, g, b):
    mu = jnp.mean(z, axis=-1, keepdims=True)
    zc = z - mu
    var = jnp.mean(zc * zc, axis=-1, keepdims=True)
    return zc * lax.rsqrt(var + LN_EPS) * g + b


def _mix_kernel(a_ref, gt_ref, u_ref, up_ref, un_ref, x_ref, wp_ref, ps_ref, wo_ref,
                g_ref, b_ref, o32_ref, o16_ref, pad_ref, *, alpha, seq):
    tm = u_ref.shape[0]
    d_model = a_ref.shape[1]
    n_groups = len(POOL_WINDOWS)
    gin = u_ref.shape[1] // n_groups
    i = pl.program_id(1)
    nblk = pl.num_programs(1)
    pad_ref[0:POOL_HALO, :] = jnp.where(i > 0, up_ref[...], 0.0)
    pad_ref[POOL_HALO:POOL_HALO + tm, :] = u_ref[...]
    pad_ref[POOL_HALO + tm:, :] = jnp.where(i < nblk - 1, un_ref[...], 0.0)

    t = i * tm + lax.broadcasted_iota(jnp.int32, (tm, 1), 0)
    n_pad = tm + 2 * POOL_HALO
    outs = []
    for g, w in enumerate(POOL_WINDOWS):
        lo_off = -(w // 2)
        hi_off = w - 1 - w // 2
        c0 = g * gin
        win = pad_ref[:, c0:c0 + gin]
        win = win + pltpu.roll(win, 1, 0)
        width = 2
        while width < w:
            half = width // 2
            win = pltpu.roll(win, n_pad - half, 0) + pltpu.roll(win, half, 0)
            width *= 2
        tot = win[POOL_HALO:POOL_HALO + tm]
        lo = jnp.maximum(t + lo_off, 0)
        hi = jnp.minimum(t + hi_off, seq - 1)
        cnt = (hi - lo + 1).astype(F32)
        pooled = tot / cnt - u_ref[:, c0:c0 + gin]
        outs.append(jnp.dot(pooled.astype(BF16), wp_ref[g], preferred_element_type=F32))
    p = jnp.concatenate(outs, axis=-1) * ps_ref[...]
    m = gt_ref[:, :d_model] * a_ref[...] + gt_ref[:, d_model:] * p
    y = jnp.dot(m.astype(BF16), wo_ref[...], preferred_element_type=F32)
    out = _layer_norm(alpha * x_ref[...] + y, g_ref[...], b_ref[...])
    o32_ref[...] = out
    o16_ref[...] = out.astype(BF16)


def _mix(a, gates, u, x, w_pool, pool_scale, w_o, ln_g, ln_b, *, layer, alpha, tm_pref=256):
    B, S, D = x.shape
    U = u.shape[-1]
    tm = _tile(S, tm_pref)
    hb = tm // POOL_HALO
    n_halo_blocks = S // POOL_HALO
    kernel = functools.partial(_mix_kernel, alpha=alpha, seq=S)
    row = lambda b, i: (b, i, 0)
    vec = pl.BlockSpec((None, 1, D), lambda b, i: (layer, 0, 0))
    return pl.pallas_call(
        kernel,
        grid=(B, S // tm),
        in_specs=[
            pl.BlockSpec((None, tm, D), row),
            pl.BlockSpec((None, tm, 2 * D), row),
            pl.BlockSpec((None, tm, U), row),
            pl.BlockSpec((None, POOL_HALO, U),
                         lambda b, i: (b, jnp.maximum(i * hb - 1, 0), 0)),
            pl.BlockSpec((None, POOL_HALO, U),
                         lambda b, i: (b, jnp.minimum((i + 1) * hb, n_halo_blocks - 1), 0)),
            pl.BlockSpec((None, tm, D), row),
            pl.BlockSpec((None,) + w_pool.shape[1:], lambda b, i: (layer, 0, 0, 0)),
            vec,
            pl.BlockSpec((None, D, D), lambda b, i: (layer, 0, 0)),
            vec,
            vec,
        ],
        out_specs=[pl.BlockSpec((None, tm, D), row), pl.BlockSpec((None, tm, D), row)],
        out_shape=[jax.ShapeDtypeStruct((B, S, D), F32), jax.ShapeDtypeStruct((B, S, D), BF16)],
        scratch_shapes=[pltpu.VMEM((tm + 2 * POOL_HALO, U), F32)],
        compiler_params=_params(("parallel", "arbitrary")),
        name="mix_wo_ln",
    )(a, gates, u, u, u, x, w_pool, pool_scale, w_o, ln_g, ln_b)


def _mlp_kernel(xb_ref, x_ref, wu_ref, wd_ref, g_ref, b_ref, o32_ref, o16_ref, acc_ref, *, alpha):
    r = pl.program_id(0)
    f = pl.program_id(1)
    n_rows = pl.num_programs(0) - 1

    def down_of_up():
        h = jnp.dot(xb_ref[...], wu_ref[...], preferred_element_type=F32)
        h = jnp.square(jnp.maximum(h, 0.0))
        return jnp.dot(h.astype(BF16), wd_ref[...], preferred_element_type=F32)

    def norm_previous():
        out = _layer_norm(alpha * x_ref[...] + acc_ref[...], g_ref[...], b_ref[...])
        o32_ref[...] = out
        o16_ref[...] = out.astype(BF16)

    @pl.when((f == 0) & (r == 0))
    def _():
        acc_ref[...] = down_of_up()

    @pl.when((f == 0) & (r > 0) & (r < n_rows))
    def _():
        norm_previous()
        acc_ref[...] = down_of_up()

    @pl.when((f == 0) & (r == n_rows))
    def _():
        norm_previous()

    @pl.when((f > 0) & (r < n_rows))
    def _():
        acc_ref[...] += down_of_up()


def _mlp(xb, x, w_up, w_down, ln_g, ln_b, *, layer, alpha, tm_pref=512, tf_pref=1024):
    B, S, D = x.shape
    F = w_up.shape[2]
    tm = _tile(S, tm_pref)
    tf = _tile(F, tf_pref)
    n_rows = B * S // tm
    kernel = functools.partial(_mlp_kernel, alpha=alpha)
    summed = lambda r, f: (jnp.minimum(r, n_rows - 1), 0)
    normed = lambda r, f: (jnp.maximum(r - 1, 0), 0)
    vec = pl.BlockSpec((None, 1, D), lambda r, f: (layer, 0, 0))
    out32, out16 = pl.pallas_call(
        kernel,
        grid=(n_rows + 1, F // tf),
        in_specs=[
            pl.BlockSpec((tm, D), summed),
            pl.BlockSpec((tm, D), normed),
            pl.BlockSpec((None, D, tf), lambda r, f: (layer, 0, f)),
            pl.BlockSpec((None, tf, D), lambda r, f: (layer, f, 0)),
            vec,
            vec,
        ],
        out_specs=[pl.BlockSpec((tm, D), normed), pl.BlockSpec((tm, D), normed)],
        out_shape=[jax.ShapeDtypeStruct((B * S, D), F32), jax.ShapeDtypeStruct((B * S, D), BF16)],
        scratch_shapes=[pltpu.VMEM((tm, D), F32)],
        compiler_params=_params(("arbitrary", "arbitrary")),
        name="mlp_ln",
    )(xb.reshape(B * S, D), x.reshape(B * S, D), w_up, w_down, ln_g, ln_b)
    return out32.reshape(B, S, D), out16.reshape(B, S, D)


def _trunk(x, w):
    B, S, D = x.shape
    depth = w["w_in"].shape[0]
    alpha = (2 * depth) ** 0.25
    n_heads = D // HEAD_DIM
    kv_w = N_KV_HEADS * HEAD_DIM
    cos, sin = _rope_tables(S)
    xb = x
    for l in range(depth):
        qk, vT, u, gates = _proj(xb, w["w_qk"], w["w_in"], w["qk_gains"], cos, sin, layer=l,
                                 v_w=kv_w, u_w=D // 2, tn=PROJ_TN)
        a = _flash(qk, vT, n_heads=n_heads)
        x, xb = _mix(a, gates, u, x, w["w_pool"], w["pool_scale"], w["w_o"],
                     w["ln1_g"], w["ln1_b"], layer=l, alpha=alpha)
        x, xb = _mlp(xb, x, w["w_up"], w["w_down"], w["ln2_g"], w["ln2_b"], layer=l, alpha=alpha)
    return x


def kernel(x_prompt, x_sample, w_in, q_norm, k_norm, w_pool, pool_scale, w_o,
           ln1_g, ln1_b, w_up, w_down, ln2_g, ln2_b):
    depth, d_model, _ = w_in.shape
    kv_w = N_KV_HEADS * HEAD_DIM
    qk_w = d_model + kv_w
    w_in_b = w_in.astype(BF16)
    gq = _permute_head_lanes(q_norm) * SOFTMAX_SCALE_LOG2
    gk = _permute_head_lanes(k_norm)
    qk_gains = jnp.concatenate(
        [jnp.broadcast_to(gq[:, None, None, :], (depth, d_model // PROJ_TN, 1, HEAD_DIM)),
         jnp.broadcast_to(gk[:, None, None, :], (depth, kv_w // PROJ_TN, 1, HEAD_DIM))], axis=1)
    row = lambda v: v.reshape(depth, 1, v.shape[-1])
    w = {
        "w_in": w_in_b, "w_qk": _permute_head_lanes(w_in_b[:, :, :qk_w]), "qk_gains": qk_gains,
        "w_pool": w_pool.astype(BF16), "pool_scale": row(pool_scale),
        "w_o": w_o.astype(BF16),
        "ln1_g": row(ln1_g), "ln1_b": row(ln1_b),
        "w_up": w_up.astype(BF16), "w_down": w_down.astype(BF16),
        "ln2_g": row(ln2_g), "ln2_b": row(ln2_b),
    }
    return (_trunk(x_prompt, w), _trunk(x_sample, w))
```

```python
import functools

import jax
import jax.numpy as jnp
from jax import lax
from jax.experimental import pallas as pl
from jax.experimental.pallas import tpu as pltpu

F32 = jnp.float32
BF16 = jnp.bfloat16

HEAD_DIM = 128
N_KV_HEADS = 4
GRID_W = 64
ROPE_THETA = 10000.0
POOL_WINDOWS = (2, 4, 8, 16)
POOL_HALO = 8
assert all(w >= 2 and w & (w - 1) == 0 and w // 2 <= POOL_HALO for w in POOL_WINDOWS)
LN_EPS = 1e-5
QK_EPS = 1e-6
LOG2E = 1.4426950408889634
SOFTMAX_SCALE_LOG2 = HEAD_DIM ** -0.5 * LOG2E
NEG_BIG = -1e30
ROPE_ROW_BLOCK = 128
PROJ_TN = 512

VMEM_LIMIT_BYTES = 56 * 1024 * 1024


def _tile(n, pref):
    t = min(n, pref)
    assert n % t == 0, (n, pref)
    return t


def _params(sem):
    return pltpu.CompilerParams(dimension_semantics=sem,
                                vmem_limit_bytes=VMEM_LIMIT_BYTES)


def _permute_head_lanes(a):
    lead = a.shape[:-1]
    n = a.shape[-1] // HEAD_DIM
    a = a.reshape(lead + (n, 2, 2, HEAD_DIM // 4))
    a = jnp.swapaxes(a, -3, -2)
    return a.reshape(lead + (n * HEAD_DIM,))


def _rope_tables(seq):
    t = jnp.arange(seq)
    row = (t // GRID_W).astype(F32)
    col = (t % GRID_W).astype(F32)
    axis_dim = HEAD_DIM // 2
    inv_freq = ROPE_THETA ** (-jnp.arange(0, axis_dim, 2, dtype=F32) / axis_dim)
    ar = row[:, None] * inv_freq[None, :]
    ac = col[:, None] * inv_freq[None, :]
    cos = jnp.concatenate([jnp.cos(ar), jnp.cos(ac), jnp.cos(ar), jnp.cos(ac)], axis=-1)
    sin = jnp.concatenate([-jnp.sin(ar), -jnp.sin(ac), jnp.sin(ar), jnp.sin(ac)], axis=-1)
    return cos, sin


def _proj_kernel(x_ref, wqk_ref, w_ref, g_ref, cos_ref, sin_ref, qk_ref, vT_ref, u_ref, gt_ref,
                 acc_ref, *maybe_xb_ref, rb, j_v, j_u, j_gate):
    tm, tn = acc_ref.shape
    j = pl.program_id(2)
    heads = [slice(h * HEAD_DIM, (h + 1) * HEAD_DIM) for h in range(tn // HEAD_DIM)]
    row_blocks = [slice(r * rb, (r + 1) * rb) for r in range(tm // rb)]

    lhs_ref = x_ref
    if maybe_xb_ref:
        (lhs_ref,) = maybe_xb_ref

        @pl.when(j == 0)
        def _():
            lhs_ref[...] = x_ref[...].astype(lhs_ref.dtype)

    def interleaved(weights_ref, epilogue):
        def block_matmul(rows):
            acc_ref[rows, :] = jnp.dot(lhs_ref[rows, :], weights_ref[...],
                                       preferred_element_type=F32)

        block_matmul(row_blocks[0])
        for r, rows in enumerate(row_blocks):
            if r + 1 < len(row_blocks):
                block_matmul(row_blocks[r + 1])
            epilogue(rows)

    def rope_epilogue(rows):
        gain = g_ref[...]
        cos = cos_ref[rows, :]
        sin = sin_ref[rows, :]
        for cols in heads:
            xh = acc_ref[rows, cols]
            ms = jnp.mean(xh * xh, axis=-1, keepdims=True)
            y = xh * lax.rsqrt(ms + QK_EPS) * gain
            rot = y * cos + pltpu.roll(y, HEAD_DIM // 2, 1) * sin
            qk_ref[rows, cols] = rot.astype(qk_ref.dtype)

    def v_epilogue(rows):
        for h, cols in enumerate(heads):
            vT_ref[h, :, rows] = acc_ref[rows, cols].T.astype(vT_ref.dtype)

    def gate_epilogue(rows):
        gt_ref[rows, :] = 1.0 / (1.0 + jnp.exp(-acc_ref[rows, :]))

    pl.when(j < j_v)(lambda: interleaved(wqk_ref, rope_epilogue))
    pl.when(j == j_v)(lambda: interleaved(w_ref, v_epilogue))
    pl.when(j >= j_gate)(lambda: interleaved(w_ref, gate_epilogue))

    @pl.when((j >= j_u) & (j < j_gate))
    def _():
        u_ref[...] = jnp.dot(lhs_ref[...], w_ref[...], preferred_element_type=F32)


def _proj(x, w_qk, w, gains, cos, sin, *, layer, v_w, u_w, tm_pref=1024, tn=512):
    B, S, D = x.shape
    N = w.shape[2]
    qk_w = w_qk.shape[2]
    assert v_w == tn and qk_w % tn == 0 and u_w % tn == 0 and N % tn == 0
    tm = _tile(S, tm_pref)
    n_i = S // tm
    rb = _tile(tm, ROPE_ROW_BLOCK)
    j_v = qk_w // tn
    j_u = j_v + 1
    j_gate = j_u + u_w // tn
    gate_w = N - j_gate * tn
    kernel = functools.partial(_proj_kernel, rb=rb, j_v=j_v, j_u=j_u, j_gate=j_gate)
    n_j = N // tn
    n_rows = B * n_i

    def following(b, i):
        r = jnp.minimum(b * n_i + i + 1, n_rows - 1)
        return r // n_i, r % n_i

    def pick(done, b, i, b_next, i_next):
        move = done & (b * n_i + i < n_rows - 1)
        return jnp.where(move, b_next, b), jnp.where(move, i_next, i), move

    if x.dtype == BF16:
        x_map = lambda b, i, j: (b, i, 0)
    else:
        def x_map(b, i, j):
            b_next, i_next = following(b, i)
            return jnp.where(j > 0, b_next, b), jnp.where(j > 0, i_next, i), 0

    def qk_map(b, i, j):
        bb, ii, move = pick(j >= j_v, b, i, *following(b, i))
        return bb, ii, jnp.where(move, 0, jnp.minimum(j, j_v - 1))

    def vT_map(b, i, j):
        bb, ii, _ = pick(j > j_v, b, i, *following(b, i))
        return bb, 0, 0, ii

    def u_map(b, i, j):
        bb, ii, move = pick(j >= j_gate, b, i, *following(b, i))
        return bb, ii, jnp.where(move, 0, jnp.minimum(jnp.maximum(j - j_u, 0), j_gate - j_u - 1))

    return pl.pallas_call(
        kernel,
        grid=(B, n_i, n_j),
        in_specs=[
            pl.BlockSpec((None, tm, D), x_map),
            pl.BlockSpec((None, D, tn), lambda b, i, j: (layer, 0, jnp.minimum(j, j_v - 1))),
            pl.BlockSpec((None, D, tn), lambda b, i, j: (layer, 0, jnp.where(j < j_v, n_j - 1, j))),
            pl.BlockSpec((None, None, 1, HEAD_DIM),
                         lambda b, i, j: (layer, jnp.minimum(j, j_v - 1), 0, 0)),
            pl.BlockSpec((tm, HEAD_DIM), lambda b, i, j: (i, 0)),
            pl.BlockSpec((tm, HEAD_DIM), lambda b, i, j: (i, 0)),
        ],
        out_specs=[
            pl.BlockSpec((None, tm, tn), qk_map),
            pl.BlockSpec((None, tn // HEAD_DIM, HEAD_DIM, tm), vT_map),
            pl.BlockSpec((None, tm, tn), u_map),
            pl.BlockSpec((None, tm, tn), lambda b, i, j: (b, i, jnp.maximum(j - j_gate, 0))),
        ],
        out_shape=[
            jax.ShapeDtypeStruct((B, S, qk_w), BF16),
            jax.ShapeDtypeStruct((B, v_w // HEAD_DIM, HEAD_DIM, S), BF16),
            jax.ShapeDtypeStruct((B, S, u_w), F32),
            jax.ShapeDtypeStruct((B, S, gate_w), F32),
        ],
        scratch_shapes=[pltpu.VMEM((tm, tn), F32)]
        + ([] if x.dtype == BF16 else [pltpu.VMEM((tm, D), BF16)]),
        compiler_params=_params(("arbitrary", "arbitrary", "arbitrary")),
        name="proj",
    )(x, w_qk, w, gains, cos, sin)


def _flash_kernel(q_ref, qn_ref, k_ref, kn_ref, vT_ref, o_ref,
                  qT_ref, qTn_ref, m_ref, l_ref, acc_ref, s_ref, p_ref, mx_ref,
                  *, tk, ts, n_chunks, gh):
    G = qT_ref.shape[0]
    tq = qT_ref.shape[2]
    nt = tk // ts

    def slot_window(g):
        sb, h = divmod(g, gh)
        return slice(sb * tq, (sb + 1) * tq), slice(h * HEAD_DIM, (h + 1) * HEAD_DIM)

    def transpose_heads(src_ref, dst_ref):
        for g in range(G):
            rows, cols = slot_window(g)
            dst_ref[g] = src_ref[rows, cols].astype(F32).T.astype(BF16)

    def sub8(x):
        return x.reshape(ts // 8, 8, tq)

    def qk_tile(qt_ref, keys_ref, off, g, t):
        kt = keys_ref[pl.ds(off + t * ts, ts), :]
        s = jnp.dot(kt, qt_ref[g], preferred_element_type=F32)
        s_ref[g, t * ts:(t + 1) * ts, :] = s
        return jnp.max(sub8(s), axis=0)

    def exp_tile(g, t, m_new):
        p = jnp.exp2(s_ref[g, t * ts:(t + 1) * ts, :] - m_new)
        p_ref[g, t * ts:(t + 1) * ts, :] = p.astype(BF16)
        return jnp.sum(sub8(p), axis=0)

    def chunk_step(off, qt_next_ref, keys_next_ref, off_next):
        for g in range(G):
            m_old = m_ref[g]
            m_new = jnp.maximum(m_old, jnp.max(mx_ref[g], axis=0, keepdims=True))
            alpha = jnp.exp2(m_old - m_new)
            m_ref[g] = m_new
            lsum = None
            mx = None
            for t in range(nt):
                ls = exp_tile(g, t, m_new)
                lsum = ls if lsum is None else lsum + ls
                tile_mx = qk_tile(qt_next_ref, keys_next_ref, off_next, g, t)
                mx = tile_mx if mx is None else jnp.maximum(mx, tile_mx)
            l_ref[g] = alpha * l_ref[g] + lsum
            acc_ref[g] = alpha * acc_ref[g] + jnp.dot(
                vT_ref[:, pl.ds(off, tk)], p_ref[g], preferred_element_type=F32)
            mx_ref[g] = mx

    m_ref[...] = jnp.full(m_ref.shape, NEG_BIG, F32)
    l_ref[...] = jnp.zeros(l_ref.shape, F32)
    acc_ref[...] = jnp.zeros(acc_ref.shape, F32)

    @pl.when((pl.program_id(0) == 0) & (pl.program_id(1) == 0) & (pl.program_id(2) == 0))
    def _():
        transpose_heads(q_ref, qT_ref)
        for g in range(G):
            mx = None
            for t in range(nt):
                tile_mx = qk_tile(qT_ref, k_ref, 0, g, t)
                mx = tile_mx if mx is None else jnp.maximum(mx, tile_mx)
            mx_ref[g] = mx

    transpose_heads(qn_ref, qTn_ref)

    def body(c, carry):
        off = pl.multiple_of(c * tk, tk)
        chunk_step(off, qT_ref, k_ref, pl.multiple_of(off + tk, tk))
        return carry

    lax.fori_loop(0, n_chunks - 1, body, 0)
    chunk_step((n_chunks - 1) * tk, qTn_ref, kn_ref, 0)

    for g in range(G):
        o = acc_ref[g] / jnp.sum(l_ref[g], axis=0, keepdims=True)
        rows, cols = slot_window(g)
        o_ref[rows, cols] = o.T.astype(o_ref.dtype)
    qT_ref[...] = qTn_ref[...]


def _flash(qk, vT, *, n_heads, tq_pref=256, nsub_pref=4, tk_pref=1024, ts_pref=128):
    B, KV, hd, S = vT.shape
    gh = n_heads // KV
    tq = _tile(S, tq_pref)
    nsub = _tile(S // tq, nsub_pref)
    tqb = nsub * tq
    G = nsub * gh
    tk = _tile(S, tk_pref)
    ts = _tile(tk, ts_pref)
    nq = S // tqb
    kernel = functools.partial(_flash_kernel, tk=tk, ts=ts, n_chunks=S // tk, gh=gh)

    def next_step(b, h, i):
        wrap = i == nq - 1
        flat = jnp.minimum(b * KV + h + wrap.astype(jnp.int32), B * KV - 1)
        last = (b * KV + h == B * KV - 1) & wrap
        return flat // KV, flat % KV, jnp.where(wrap & ~last, 0, jnp.minimum(i + 1, nq - 1))

    def next_q(b, h, i):
        nb, nh, ni = next_step(b, h, i)
        return nb, ni, nh

    def next_k(b, h, i):
        nb, nh, _ = next_step(b, h, i)
        return nb, 0, n_heads + nh

    return pl.pallas_call(
        kernel,
        grid=(B, KV, nq),
        in_specs=[
            pl.BlockSpec((None, tqb, gh * hd), lambda b, h, i: (b, i, h)),
            pl.BlockSpec((None, tqb, gh * hd), next_q),
            pl.BlockSpec((None, S, hd), lambda b, h, i: (b, 0, n_heads + h)),
            pl.BlockSpec((None, tk, hd), next_k),
            pl.BlockSpec((None, None, hd, S), lambda b, h, i: (b, h, 0, 0)),
        ],
        out_specs=pl.BlockSpec((None, tqb, gh * hd), lambda b, h, i: (b, i, h)),
        out_shape=jax.ShapeDtypeStruct((B, S, n_heads * hd), F32),
        scratch_shapes=[
            pltpu.VMEM((G, hd, tq), BF16),
            pltpu.VMEM((G, hd, tq), BF16),
            pltpu.VMEM((G, 1, tq), F32),
            pltpu.VMEM((G, 8, tq), F32),
            pltpu.VMEM((G, hd, tq), F32),
            pltpu.VMEM((G, tk, tq), F32),
            pltpu.VMEM((G, tk, tq), BF16),
            pltpu.VMEM((G, 8, tq), F32),
        ],
        compiler_params=_params(("arbitrary", "arbitrary", "arbitrary")),
        name="flash_attn",
    )(qk, qk, qk, qk, vT)


def _layer_norm(z, g, b):
    mu = jnp.mean(z, axis=-1, keepdims=True)
    zc = z - mu
    var = jnp.mean(zc * zc, axis=-1, keepdims=True)
    return zc * lax.rsqrt(var + LN_EPS) * g + b


def _mix_kernel(a_ref, gt_ref, u_ref, up_ref, un_ref, x_ref, wp_ref, ps_ref, wo_ref,
                g_ref, b_ref, o32_ref, o16_ref, pad_ref, *, alpha, seq):
    tm = u_ref.shape[0]
    d_model = a_ref.shape[1]
    n_groups = len(POOL_WINDOWS)
    gin = u_ref.shape[1] // n_groups
    i = pl.program_id(1)
    nblk = pl.num_programs(1)
    pad_ref[0:POOL_HALO, :] = jnp.where(i > 0, up_ref[...], 0.0)
    pad_ref[POOL_HALO:POOL_HALO + tm, :] = u_ref[...]
    pad_ref[POOL_HALO + tm:, :] = jnp.where(i < nblk - 1, un_ref[...], 0.0)

    t = i * tm + lax.broadcasted_iota(jnp.int32, (tm, 1), 0)
    n_pad = tm + 2 * POOL_HALO
    outs = []
    for g, w in enumerate(POOL_WINDOWS):
        lo_off = -(w // 2)
        hi_off = w - 1 - w // 2
        c0 = g * gin
        win = pad_ref[:, c0:c0 + gin]
        win = win + pltpu.roll(win, 1, 0)
        width = 2
        while width < w:
            half = width // 2
            win = pltpu.roll(win, n_pad - half, 0) + pltpu.roll(win, half, 0)
            width *= 2
        tot = win[POOL_HALO:POOL_HALO + tm]
        lo = jnp.maximum(t + lo_off, 0)
        hi = jnp.minimum(t + hi_off, seq - 1)
        cnt = (hi - lo + 1).astype(F32)
        pooled = tot / cnt - u_ref[:, c0:c0 + gin]
        outs.append(jnp.dot(pooled.astype(BF16), wp_ref[g], preferred_element_type=F32))
    p = jnp.concatenate(outs, axis=-1) * ps_ref[...]
    m = gt_ref[:, :d_model] * a_ref[...] + gt_ref[:, d_model:] * p
    y = jnp.dot(m.astype(BF16), wo_ref[...], preferred_element_type=F32)
    out = _layer_norm(alpha * x_ref[...] + y, g_ref[...], b_ref[...])
    o32_ref[...] = out
    o16_ref[...] = out.astype(BF16)


def _mix(a, gates, u, x, w_pool, pool_scale, w_o, ln_g, ln_b, *, layer, alpha, tm_pref=256):
    B, S, D = x.shape
    U = u.shape[-1]
    tm = _tile(S, tm_pref)
    hb = tm // POOL_HALO
    n_halo_blocks = S // POOL_HALO
    kernel = functools.partial(_mix_kernel, alpha=alpha, seq=S)
    row = lambda b, i: (b, i, 0)
    vec = pl.BlockSpec((None, 1, D), lambda b, i: (layer, 0, 0))
    return pl.pallas_call(
        kernel,
        grid=(B, S // tm),
        in_specs=[
            pl.BlockSpec((None, tm, D), row),
            pl.BlockSpec((None, tm, 2 * D), row),
            pl.BlockSpec((None, tm, U), row),
            pl.BlockSpec((None, POOL_HALO, U),
                         lambda b, i: (b, jnp.maximum(i * hb - 1, 0), 0)),
            pl.BlockSpec((None, POOL_HALO, U),
                         lambda b, i: (b, jnp.minimum((i + 1) * hb, n_halo_blocks - 1), 0)),
            pl.BlockSpec((None, tm, D), row),
            pl.BlockSpec((None,) + w_pool.shape[1:], lambda b, i: (layer, 0, 0, 0)),
            vec,
            pl.BlockSpec((None, D, D), lambda b, i: (layer, 0, 0)),
            vec,
            vec,
        ],
        out_specs=[pl.BlockSpec((None, tm, D), row), pl.BlockSpec((None, tm, D), row)],
        out_shape=[jax.ShapeDtypeStruct((B, S, D), F32), jax.ShapeDtypeStruct((B, S, D), BF16)],
        scratch_shapes=[pltpu.VMEM((tm + 2 * POOL_HALO, U), F32)],
        compiler_params=_params(("parallel", "arbitrary")),
        name="mix_wo_ln",
    )(a, gates, u, u, u, x, w_pool, pool_scale, w_o, ln_g, ln_b)


def _mlp_kernel(xb_ref, x_ref, wu_ref, wd_ref, g_ref, b_ref, o32_ref, o16_ref, acc_ref, *, alpha):
    r = pl.program_id(0)
    f = pl.program_id(1)
    n_rows = pl.num_programs(0) - 1

    def down_of_up():
        h = jnp.dot(xb_ref[...], wu_ref[...], preferred_element_type=F32)
        h = jnp.square(jnp.maximum(h, 0.0))
        return jnp.dot(h.astype(BF16), wd_ref[...], preferred_element_type=F32)

    def norm_previous():
        out = _layer_norm(alpha * x_ref[...] + acc_ref[...], g_ref[...], b_ref[...])
        o32_ref[...] = out
        o16_ref[...] = out.astype(BF16)

    @pl.when((f == 0) & (r == 0))
    def _():
        acc_ref[...] = down_of_up()

    @pl.when((f == 0) & (r > 0) & (r < n_rows))
    def _():
        norm_previous()
        acc_ref[...] = down_of_up()

    @pl.when((f == 0) & (r == n_rows))
    def _():
        norm_previous()

    @pl.when((f > 0) & (r < n_rows))
    def _():
        acc_ref[...] += down_of_up()


def _mlp(xb, x, w_up, w_down, ln_g, ln_b, *, layer, alpha, tm_pref=512, tf_pref=1024):
    B, S, D = x.shape
    F = w_up.shape[2]
    tm = _tile(S, tm_pref)
    tf = _tile(F, tf_pref)
    n_rows = B * S // tm
    kernel = functools.partial(_mlp_kernel, alpha=alpha)
    summed = lambda r, f: (jnp.minimum(r, n_rows - 1), 0)
    normed = lambda r, f: (jnp.where(f == 0, jnp.maximum(r - 1, 0), jnp.minimum(r, n_rows - 1)), 0)
    vec = pl.BlockSpec((None, 1, D), lambda r, f: (layer, 0, 0))
    out32, out16 = pl.pallas_call(
        kernel,
        grid=(n_rows + 1, F // tf),
        in_specs=[
            pl.BlockSpec((tm, D), summed),
            pl.BlockSpec((tm, D), normed),
            pl.BlockSpec((None, D, tf), lambda r, f: (layer, 0, f)),
            pl.BlockSpec((None, tf, D), lambda r, f: (layer, f, 0)),
            vec,
            vec,
        ],
        out_specs=[pl.BlockSpec((tm, D), normed), pl.BlockSpec((tm, D), normed)],
        out_shape=[jax.ShapeDtypeStruct((B * S, D), F32), jax.ShapeDtypeStruct((B * S, D), BF16)],
        scratch_shapes=[pltpu.VMEM((tm, D), F32)],
        compiler_params=_params(("arbitrary", "arbitrary")),
        name="mlp_ln",
    )(xb.reshape(B * S, D), x.reshape(B * S, D), w_up, w_down, ln_g, ln_b)
    return out32.reshape(B, S, D), out16.reshape(B, S, D)


def _trunk(x, w):
    B, S, D = x.shape
    depth = w["w_in"].shape[0]
    alpha = (2 * depth) ** 0.25
    n_heads = D // HEAD_DIM
    kv_w = N_KV_HEADS * HEAD_DIM
    cos, sin = _rope_tables(S)
    xb = x
    for l in range(depth):
        qk, vT, u, gates = _proj(xb, w["w_qk"], w["w_in"], w["qk_gains"], cos, sin, layer=l,
                                 v_w=kv_w, u_w=D // 2, tn=PROJ_TN)
        a = _flash(qk, vT, n_heads=n_heads)
        x, xb = _mix(a, gates, u, x, w["w_pool"], w["pool_scale"], w["w_o"],
                     w["ln1_g"], w["ln1_b"], layer=l, alpha=alpha)
        x, xb = _mlp(xb, x, w["w_up"], w["w_down"], w["ln2_g"], w["ln2_b"], layer=l, alpha=alpha)
    return x


def kernel(x_prompt, x_sample, w_in, q_norm, k_norm, w_pool, pool_scale, w_o,
           ln1_g, ln1_b, w_up, w_down, ln2_g, ln2_b):
    depth, d_model, _ = w_in.shape
    kv_w = N_KV_HEADS * HEAD_DIM
    qk_w = d_model + kv_w
    w_in_b = w_in.astype(BF16)
    gq = _permute_head_lanes(q_norm) * SOFTMAX_SCALE_LOG2
    gk = _permute_head_lanes(k_norm)
    qk_gains = jnp.concatenate(
        [jnp.broadcast_to(gq[:, None, None, :], (depth, d_model // PROJ_TN, 1, HEAD_DIM)),
         jnp.broadcast_to(gk[:, None, None, :], (depth, kv_w // PROJ_TN, 1, HEAD_DIM))], axis=1)
    row = lambda v: v.reshape(depth, 1, v.shape[-1])
    w = {
        "w_in": w_in_b, "w_qk": _permute_head_lanes(w_in_b[:, :, :qk_w]), "qk_gains": qk_gains,
        "w_pool": w_pool.astype(BF16), "pool_scale": row(pool_scale),
        "w_o": w_o.astype(BF16),
        "ln1_g": row(ln1_g), "ln1_b": row(ln1_b),
        "w_up": w_up.astype(BF16), "w_down": w_down.astype(BF16),
        "ln2_g": row(ln2_g), "ln2_b": row(ln2_b),
    }
    return (_trunk(x_prompt, w), _trunk(x_sample, w))
```
